```python
import jax
import jax.numpy as jnp
from jax import lax
import numpy as np

D_MODEL = 1024
BATCH = 16
SEQ = 4096
DEPTH = 1

CHUNK = 64
Q_BLOCK = 2 * CHUNK
FOX_HEADS = 8
FOX_HEAD_DIM = 64
FOX_WIDTH = FOX_HEADS * FOX_HEAD_DIM
RWKV_HEADS = 8
RWKV_HEAD_DIM = 64
RWKV_WIDTH = RWKV_HEADS * RWKV_HEAD_DIM
DECAY_LORA = 64
AAA_LORA = 64
GATE_LORA = 128
D_FF = 2816
CONV_WIDTH = 3
N_BRANCHES = 2
N_MOD = 6
RMS_EPS = 1e-6
GN_EPS = RWKV_HEAD_DIM * 1e-5
L2_EPS = 1e-12

FOX_SPLITS = (FOX_WIDTH, FOX_WIDTH, FOX_WIDTH, FOX_HEADS)
RWKV_SPLITS = (RWKV_WIDTH, RWKV_WIDTH, RWKV_WIDTH, DECAY_LORA, AAA_LORA, GATE_LORA)
FOX_COLS = 3 * FOX_WIDTH + FOX_HEADS
RWKV_COLS = 3 * RWKV_WIDTH + DECAY_LORA + AAA_LORA + GATE_LORA
GATE_COLS = N_BRANCHES * D_MODEL
IN_COLS = FOX_COLS + RWKV_COLS + GATE_COLS

kernel_name = 'fox_rwkv7_hybrid_block'


def _split(t, sizes):
    offs = np.cumsum(sizes)[:-1].tolist()
    return jnp.split(t, offs, axis=-1)


def rms_norm(x, gain):
    xf = x.astype(jnp.float32)
    y = xf * lax.rsqrt(jnp.mean(xf * xf, axis=-1, keepdims=True) + RMS_EPS)
    return (y * gain).astype(x.dtype)


def modulate(xn, shift, scale):
    return xn * (1 + scale[:, None, :]) + shift[:, None, :]


def token_shift(t):
    return jnp.pad(t[:, :-1], ((0, 0), (1, 0), (0, 0)))


def fox_attention(q, k, v, log_f):
    seq = q.shape[2]
    cum = jnp.cumsum(log_f, axis=-1)
    scale = FOX_HEAD_DIM ** -0.5
    outs = []
    for blk in range(seq // Q_BLOCK):
        q0 = blk * Q_BLOCK
        q1 = q0 + Q_BLOCK
        logits = jnp.einsum('bhqd,bhkd->bhqk', q[:, :, q0:q1], k[:, :, :q1]).astype(jnp.float32) * scale
        logits = logits + cum[:, :, q0:q1, None] - cum[:, :, None, :q1]
        causal = jnp.arange(q1)[None, :] <= (q0 + jnp.arange(Q_BLOCK))[:, None]
        logits = jnp.where(causal, logits, -jnp.inf)
        probs = jax.nn.softmax(logits, axis=-1).astype(v.dtype)
        outs.append(jnp.einsum('bhqk,bhkd->bhqd', probs, v[:, :, :q1]))
    return jnp.concatenate(outs, axis=2)


def rwkv7_time_mix(p, mu, w0, w2, a0, a2, g2, k_k, k_a, r_k, ln_w, ln_b):
    bsz, seq, _ = p.shape
    p = p + (token_shift(p) - p) * mu
    r, k, v, xw, xa, xg = _split(p, RWKV_SPLITS)
    w_log = -jax.nn.softplus(-(w0 + jnp.tanh(xw) @ w2).astype(jnp.float32)) - 0.5
    decay = jnp.exp(-jnp.exp(w_log))
    a = jax.nn.sigmoid((a0 + xa @ a2).astype(jnp.float32))
    g = jax.nn.sigmoid(xg) @ g2

    def heads(t):
        return t.astype(jnp.float32).reshape(bsz, seq, RWKV_HEADS, RWKV_HEAD_DIM)

    kk = heads(k * k_k)
    kk = kk / jnp.maximum(jnp.sqrt(jnp.sum(kk * kk, axis=-1, keepdims=True)), L2_EPS)
    k_mod = k.astype(jnp.float32) * (1 + (a - 1) * k_a)
    r_h, k_h, v_h, w_h, a_h = heads(r), heads(k_mod), heads(v), heads(decay), heads(a)
    a_vec = -kk
    b_vec = kk * a_h

    def step(state, inp):
        r_t, w_t, k_t, v_t, a_t, b_t = inp
        sa = jnp.einsum('bhvk,bhk->bhv', state, a_t)
        state = state * w_t[:, :, None, :] + sa[..., None] * b_t[:, :, None, :] + v_t[..., None] * k_t[:, :, None, :]
        y_t = jnp.einsum('bhvk,bhk->bhv', state, r_t)
        return state, y_t

    tm = lambda t: jnp.moveaxis(t, 1, 0)
    state0 = jnp.zeros((bsz, RWKV_HEADS, RWKV_HEAD_DIM, RWKV_HEAD_DIM), jnp.float32)
    _, y = lax.scan(step, state0, (tm(r_h), tm(w_h), tm(k_h), tm(v_h), tm(a_vec), tm(b_vec)))
    y = jnp.moveaxis(y, 0, 1)
    mean = jnp.mean(y, axis=-1, keepdims=True)
    var = jnp.mean(jnp.square(y - mean), axis=-1, keepdims=True)
    y = ((y - mean) * lax.rsqrt(var + GN_EPS)).reshape(bsz, seq, RWKV_WIDTH) * ln_w + ln_b
    bonus = (jnp.sum(r_h * k_h * r_k, axis=-1, keepdims=True) * v_h).reshape(bsz, seq, RWKV_WIDTH)
    return ((y + bonus) * g).astype(p.dtype)


def token_mixer(u, w_in, fox_b_f, fox_w_out, rwkv_mu, rwkv_w0, rwkv_w2, rwkv_a0, rwkv_a2, rwkv_g2,
                rwkv_k_k, rwkv_k_a, rwkv_r_k, rwkv_ln_w, rwkv_ln_b, rwkv_w_out, w_o):
    bsz, seq, _ = u.shape
    proj = u @ w_in
    p_fox, p_rwkv, p_gate = _split(proj, (FOX_COLS, RWKV_COLS, GATE_COLS))
    q, k, v, f = _split(p_fox, FOX_SPLITS)
    to_heads = lambda t: t.reshape(bsz, seq, FOX_HEADS, FOX_HEAD_DIM).transpose(0, 2, 1, 3)
    log_f = jax.nn.log_sigmoid((f + fox_b_f).astype(jnp.float32)).transpose(0, 2, 1)
    o_fox = fox_attention(to_heads(q), to_heads(k), to_heads(v), log_f)
    o_fox = o_fox.transpose(0, 2, 1, 3).reshape(bsz, seq, FOX_WIDTH)
    o_rwkv = rwkv7_time_mix(p_rwkv, rwkv_mu, rwkv_w0, rwkv_w2, rwkv_a0, rwkv_a2, rwkv_g2,
                            rwkv_k_k, rwkv_k_a, rwkv_r_k, rwkv_ln_w, rwkv_ln_b)
    gate_fox, gate_rwkv = jnp.split(jax.nn.sigmoid(p_gate), N_BRANCHES, axis=-1)
    merged = gate_fox * (o_fox @ fox_w_out) + gate_rwkv * (o_rwkv @ rwkv_w_out)
    return merged @ w_o


def conv_ffn(u, w_up, conv_w, conv_b, w_down):
    a, b = jnp.split(u @ w_up, 2, axis=-1)
    a = lax.conv_general_dilated(a, conv_w[:, None, :], window_strides=(1,), padding=[(CONV_WIDTH - 1, 0)],
                                 dimension_numbers=('NWC', 'WIO', 'NWC'), feature_group_count=D_FF) + conv_b
    return (jax.nn.silu(a) * b) @ w_down


def setup_inputs(seed: int = 0) -> dict:
    key = jax.random.key(seed)
    ks = jax.random.split(key, 32)
    L = DEPTH
    f32 = jnp.float32
    nrm = lambda kk, shape, s: jax.random.normal(kk, shape, f32) * s
    return {
        'x': nrm(ks[0], (BATCH, SEQ, D_MODEL), 1.0),
        'c': nrm(ks[1], (BATCH, D_MODEL), 1.0),
        'w_mod': nrm(ks[2], (L, D_MODEL, N_MOD * D_MODEL), D_MODEL ** -0.5),
        'b_mod': nrm(ks[3], (L, N_MOD * D_MODEL), 0.02),
        'ln1_g': 1.0 + nrm(ks[4], (L, D_MODEL), 0.02),
        'w_in': nrm(ks[5], (L, D_MODEL, IN_COLS), D_MODEL ** -0.5),
        'fox_b_f': 2.0 + nrm(ks[6], (L, FOX_HEADS), 0.1),
        'fox_w_out': nrm(ks[7], (L, FOX_WIDTH, D_MODEL), FOX_WIDTH ** -0.5),
        'rwkv_mu': jax.random.uniform(ks[8], (L, RWKV_COLS), f32, 0.0, 1.0),
        'rwkv_w0': jax.random.uniform(ks[9], (L, RWKV_WIDTH), f32, -4.0, 0.0),
        'rwkv_w2': nrm(ks[10], (L, DECAY_LORA, RWKV_WIDTH), 0.1 * DECAY_LORA ** -0.5),
        'rwkv_a0': nrm(ks[11], (L, RWKV_WIDTH), 0.1),
        'rwkv_a2': nrm(ks[12], (L, AAA_LORA, RWKV_WIDTH), AAA_LORA ** -0.5),
        'rwkv_g2': nrm(ks[13], (L, GATE_LORA, RWKV_WIDTH), GATE_LORA ** -0.5),
        'rwkv_k_k': 0.85 + nrm(ks[14], (L, RWKV_WIDTH), 0.05),
        'rwkv_k_a': 1.0 + nrm(ks[15], (L, RWKV_WIDTH), 0.05),
        'rwkv_r_k': nrm(ks[16], (L, RWKV_HEADS, RWKV_HEAD_DIM), 0.1),
        'rwkv_ln_w': 1.0 + nrm(ks[17], (L, RWKV_WIDTH), 0.02),
        'rwkv_ln_b': nrm(ks[18], (L, RWKV_WIDTH), 0.02),
        'rwkv_w_out': nrm(ks[19], (L, RWKV_WIDTH, D_MODEL), RWKV_WIDTH ** -0.5),
        'w_o': nrm(ks[20], (L, D_MODEL, D_MODEL), D_MODEL ** -0.5),
        'ln2_g': 1.0 + nrm(ks[21], (L, D_MODEL), 0.02),
        'w_up': nrm(ks[22], (L, D_MODEL, 2 * D_FF), D_MODEL ** -0.5),
        'conv_w': nrm(ks[23], (L, CONV_WIDTH, D_FF), CONV_WIDTH ** -0.5),
        'conv_b': nrm(ks[24], (L, D_FF), 0.02),
        'w_down': nrm(ks[25], (L, D_FF, D_MODEL), D_FF ** -0.5),
        'final_g': 1.0 + nrm(ks[26], (D_MODEL,), 0.02),
    }


def reference(x, c, w_mod, b_mod, ln1_g, w_in, fox_b_f, fox_w_out, rwkv_mu, rwkv_w0, rwkv_w2, rwkv_a0,
              rwkv_a2, rwkv_g2, rwkv_k_k, rwkv_k_a, rwkv_r_k, rwkv_ln_w, rwkv_ln_b, rwkv_w_out, w_o,
              ln2_g, w_up, conv_w, conv_b, w_down, final_g):
    h = x
    for l in range(DEPTH):
        mod = jax.nn.silu(c) @ w_mod[l] + b_mod[l]
        shift1, scale1, gate1, shift2, scale2, gate2 = jnp.split(mod, N_MOD, axis=-1)
        u = modulate(rms_norm(h, ln1_g[l]), shift1, scale1)
        mix = token_mixer(u, w_in[l], fox_b_f[l], fox_w_out[l], rwkv_mu[l], rwkv_w0[l], rwkv_w2[l],
                          rwkv_a0[l], rwkv_a2[l], rwkv_g2[l], rwkv_k_k[l], rwkv_k_a[l], rwkv_r_k[l],
                          rwkv_ln_w[l], rwkv_ln_b[l], rwkv_w_out[l], w_o[l])
        h = h + gate1[:, None, :] * mix
        u2 = modulate(rms_norm(h, ln2_g[l]), shift2, scale2)
        h = h + gate2[:, None, :] * conv_ffn(u2, w_up[l], conv_w[l], conv_b[l], w_down[l])
    return rms_norm(h, final_g)
```

```python
import functools

import jax
import jax.numpy as jnp
from jax import lax
from jax.experimental import pallas as pl
from jax.experimental.pallas import tpu as pltpu

D_MODEL = 1024
HEADS = 8
HEAD_DIM = 64
WIDTH = HEADS * HEAD_DIM
LORA_PAD = 128
GATE_LORA = 128
D_FF = 2816
RMS_EPS = 1e-6
GN_EPS = HEAD_DIM * 1e-5
L2_EPS = 1e-12
LANES = 128
CHUNK = 64

OFF_QKV = 0
OFF_F = 3 * WIDTH
OFF_RWKV = OFF_F + LANES
RWKV_P = 3 * WIDTH + 2 * LORA_PAD + GATE_LORA
OFF_GATE = OFF_RWKV + RWKV_P
IN_P = OFF_GATE + 2 * D_MODEL

VMEM_LIMIT = 56 * 1024 * 1024

HI = lax.Precision.HIGHEST
NT = (((1,), (1,)), ((), ()))
TN = (((0,), (0,)), ((), ()))


def _bdot(a, b):
    return jnp.dot(a.astype(jnp.bfloat16), b.astype(jnp.bfloat16), preferred_element_type=jnp.float32)


def _sigmoid(x):
    return 1.0 / (1.0 + jnp.exp(-x))


def _softplus(x):
    return jnp.maximum(x, 0.0) + jnp.log(1.0 + jnp.exp(-jnp.abs(x)))


def _tri(n, strict):
    r = lax.broadcasted_iota(jnp.int32, (n, n), 0)
    c = lax.broadcasted_iota(jnp.int32, (n, n), 1)
    return (r > c) if strict else (r >= c)


def _head_sum(x):
    lane = lax.broadcasted_iota(jnp.int32, (x.shape[0], LANES), 1)
    lo = lane < HEAD_DIM
    outs = []
    for g in range(x.shape[1] // LANES):
        xg = x[:, g * LANES:(g + 1) * LANES]
        s0 = jnp.sum(jnp.where(lo, xg, 0.0), axis=-1, keepdims=True)
        s1 = jnp.sum(jnp.where(lo, 0.0, xg), axis=-1, keepdims=True)
        outs.append(jnp.where(lo, s0, s1))
    return jnp.concatenate(outs, axis=-1)


def _mod_kernel(c_ref, w_ref, b_ref, o_ref):
    c = c_ref[...]
    sc = c * _sigmoid(c)
    o_ref[...] = jnp.dot(sc, w_ref[...], precision=HI, preferred_element_type=jnp.float32) + b_ref[...]


def _mod_call(c, w_mod, b_mod):
    bsz = c.shape[0]
    n = w_mod.shape[1]
    tn = 1024
    return pl.pallas_call(
        _mod_kernel,
        grid=(n // tn,),
        in_specs=[pl.BlockSpec((bsz, D_MODEL), lambda j: (0, 0)),
                  pl.BlockSpec((D_MODEL, tn), lambda j: (0, j)),
                  pl.BlockSpec((1, tn), lambda j: (0, j))],
        out_specs=pl.BlockSpec((bsz, tn), lambda j: (0, j)),
        out_shape=jax.ShapeDtypeStruct((bsz, n), jnp.float32),
        name="mod",
    )(c, w_mod, b_mod.reshape(1, n))


def _inproj_kernel(x_ref, mod_ref, g_ref, bf_ref, w_ref,
                   q_ref, k_ref, v_ref, cum_ref, cumt_ref, p_ref, gate_ref, carry_ref):
    s = pl.program_id(1)
    tm = x_ref.shape[1]

    @pl.when(s == 0)
    def _():
        carry_ref[...] = jnp.zeros_like(carry_ref)

    x = x_ref[0]
    mod = mod_ref[0]
    shift = mod[:, 0:D_MODEL]
    scale = mod[:, D_MODEL:2 * D_MODEL]
    y = x * lax.rsqrt(jnp.mean(x * x, axis=-1, keepdims=True) + RMS_EPS) * g_ref[...]
    u = (y * (1.0 + scale) + shift).astype(jnp.bfloat16)

    def proj(off, n):
        return jnp.dot(u, w_ref[:, off:off + n], preferred_element_type=jnp.float32)

    q_ref[0] = (proj(OFF_QKV, WIDTH) * (HEAD_DIM ** -0.5)).astype(jnp.bfloat16)
    k_ref[0] = proj(OFF_QKV + WIDTH, WIDTH).astype(jnp.bfloat16)
    v_ref[0] = proj(OFF_QKV + 2 * WIDTH, WIDTH).astype(jnp.bfloat16)

    z = proj(OFF_F, LANES) + bf_ref[...]
    logf = jnp.minimum(z, 0.0) - jnp.log(1.0 + jnp.exp(-jnp.abs(z)))
    tri = _tri(tm, strict=False).astype(jnp.float32)
    cum = jnp.dot(tri, logf, precision=HI, preferred_element_type=jnp.float32) + carry_ref[...]
    carry_ref[...] = cum[tm - 1:tm, :]
    cum_ref[0] = cum
    cumt_ref[0] = cum.T[0:HEADS, :]

    for off in range(0, RWKV_P, 384):
        p_ref[0, :, off:off + 384] = proj(OFF_RWKV + off, 384)
    for off in range(0, 2 * D_MODEL, 512):
        gate_ref[0, :, off:off + 512] = _sigmoid(proj(OFF_GATE + off, 512)).astype(jnp.bfloat16)


def _inproj_call(x, mod3, ln1_g, bf_pad, w_in_p, tm):
    bsz, seq, _ = x.shape
    grid = (bsz, seq // tm)
    tok = lambda n: pl.BlockSpec((1, tm, n), lambda b, s: (b, s, 0))
    const = lambda shape: pl.BlockSpec(shape, lambda b, s: (0,) * len(shape))
    out_shapes = (
        jax.ShapeDtypeStruct((bsz, seq, WIDTH), jnp.bfloat16),
        jax.ShapeDtypeStruct((bsz, seq, WIDTH), jnp.bfloat16),
        jax.ShapeDtypeStruct((bsz, seq, WIDTH), jnp.bfloat16),
        jax.ShapeDtypeStruct((bsz, seq, LANES), jnp.float32),
        jax.ShapeDtypeStruct((bsz, HEADS, seq), jnp.float32),
        jax.ShapeDtypeStruct((bsz, seq, RWKV_P), jnp.float32),
        jax.ShapeDtypeStruct((bsz, seq, 2 * D_MODEL), jnp.bfloat16),
    )
    return pl.pallas_call(
        _inproj_kernel,
        grid=grid,
        in_specs=[tok(D_MODEL),
                  pl.BlockSpec((1, 1, 6 * D_MODEL), lambda b, s: (b, 0, 0)),
                  const((1, D_MODEL)), const((1, LANES)), const((D_MODEL, IN_P))],
        out_specs=(tok(WIDTH), tok(WIDTH), tok(WIDTH), tok(LANES),
                   pl.BlockSpec((1, HEADS, tm), lambda b, s: (b, 0, s)),
                   tok(RWKV_P), tok(2 * D_MODEL)),
        out_shape=out_shapes,
        scratch_shapes=[pltpu.VMEM((1, LANES), jnp.float32)],
        compiler_params=pltpu.CompilerParams(
            dimension_semantics=("arbitrary", "arbitrary"), vmem_limit_bytes=VMEM_LIMIT),
        name="in_proj",
    )(x, mod3, ln1_g, bf_pad, w_in_p)


NEG = -1e30


def _fox_kernel(q_ref, k_ref, v_ref, cq_ref, ck_ref, o_ref, *, tq):
    pair = pl.program_id(1)
    i = pl.program_id(2)
    q = q_ref[0]
    lane = lax.broadcasted_iota(jnp.int32, (tq, LANES), 1)
    lo = lane < HEAD_DIM
    sel_r = lax.broadcasted_iota(jnp.int32, (LANES, LANES), 0)
    sel_c = lax.broadcasted_iota(jnp.int32, (LANES, LANES), 1)
    sel = (sel_r == 2 * pair + (sel_c >= HEAD_DIM).astype(jnp.int32)).astype(jnp.float32)
    cq_pair = jnp.dot(cq_ref[0], sel, precision=HI, preferred_element_type=jnp.float32)
    causal = _tri(tq, strict=False)
    zero = jnp.zeros_like(q)

    outs = []
    for hh in range(2):
        qm = jnp.where(lo, q, zero) if hh == 0 else jnp.where(lo, zero, q)
        cq = cq_pair[:, hh * HEAD_DIM:hh * HEAD_DIM + 1]

        def block(j, carry, masked):
            m, l, acc = carry
            j0 = pl.multiple_of(j * tq, tq)
            kb = k_ref[0, pl.ds(j0, tq), :]
            vb = v_ref[0, pl.ds(j0, tq), :]
            ck = ck_ref[0, 0, hh:hh + 1, pl.ds(j0, tq)]
            sc = lax.dot_general(qm, kb, NT, preferred_element_type=jnp.float32) + cq - ck
            if masked:
                sc = jnp.where(causal, sc, NEG)
            m_new = jnp.maximum(m, jnp.max(sc, axis=-1, keepdims=True))
            alpha = jnp.exp(m - m_new)
            p = jnp.exp(sc - m_new)
            l = alpha * l + jnp.sum(p, axis=-1, keepdims=True)
            acc = alpha * acc + jnp.dot(p.astype(jnp.bfloat16), vb, preferred_element_type=jnp.float32)
            return m_new, l, acc

        init = (jnp.full((tq, 1), NEG, jnp.float32), jnp.zeros((tq, 1), jnp.float32),
                jnp.zeros((tq, LANES), jnp.float32))
        carry = lax.fori_loop(0, i, functools.partial(block, masked=False), init)
        m, l, acc = block(i, carry, masked=True)
        outs.append(acc / l)
    o_ref[0] = jnp.where(lo, outs[0], outs[1]).astype(o_ref.dtype)


def _fox_call(q, k, v, cum, cumt, tq):
    bsz, seq, _ = q.shape
    npair = HEADS // 2
    cumt4 = cumt.reshape(bsz, npair, 2, seq)
    return pl.pallas_call(
        functools.partial(_fox_kernel, tq=tq),
        grid=(bsz, npair, seq // tq),
        in_specs=[pl.BlockSpec((1, tq, LANES), lambda b, p, i: (b, i, p)),
                  pl.BlockSpec((1, seq, LANES), lambda b, p, i: (b, 0, p)),
                  pl.BlockSpec((1, seq, LANES), lambda b, p, i: (b, 0, p)),
                  pl.BlockSpec((1, tq, LANES), lambda b, p, i: (b, i, 0)),
                  pl.BlockSpec((1, 1, 2, seq), lambda b, p, i: (b, p, 0, 0))],
        out_specs=pl.BlockSpec((1, tq, LANES), lambda b, p, i: (b, i, p)),
        out_shape=jax.ShapeDtypeStruct((bsz, seq, WIDTH), jnp.bfloat16),
        compiler_params=pltpu.CompilerParams(
            dimension_semantics=("arbitrary", "arbitrary", "arbitrary"), vmem_limit_bytes=VMEM_LIMIT),
        name="fox_attn",
    )(q, k, v, cum, cumt4)


def _rwkv_kernel(p_ref, mu_ref, w0_ref, w2_ref, a0_ref, a2_ref, g2_ref, kk_ref, ka_ref, rk_ref,
                 lnw_ref, lnb_ref, o_ref,
                 st_ref, carry_ref, lhs_ref, rhs_ref, v_ref, wc_ref, y_ref):
    s = pl.program_id(1)
    tm = p_ref.shape[1]
    nchunk = tm // CHUNK

    @pl.when(s == 0)
    def _():
        carry_ref[...] = jnp.zeros_like(carry_ref)
        st_ref[...] = jnp.zeros_like(st_ref)

    p = p_ref[0]
    row = lax.broadcasted_iota(jnp.int32, p.shape, 0)
    prev = jnp.where(row == 0, carry_ref[...], pltpu.roll(p, 1, 0))
    carry_ref[...] = p[tm - 1:tm, :]
    xl = p + (prev - p) * mu_ref[...]
    r = xl[:, 0:WIDTH]
    k = xl[:, WIDTH:2 * WIDTH]
    v = xl[:, 2 * WIDTH:3 * WIDTH]
    xw = xl[:, 3 * WIDTH:3 * WIDTH + LORA_PAD]
    xa = xl[:, 3 * WIDTH + LORA_PAD:3 * WIDTH + 2 * LORA_PAD]
    xg = xl[:, 3 * WIDTH + 2 * LORA_PAD:RWKV_P]

    zw = w0_ref[...] + _bdot(jnp.tanh(xw), w2_ref[...])
    lw = -jnp.exp(-_softplus(-zw) - 0.5)
    al = _sigmoid(a0_ref[...] + _bdot(xa, a2_ref[...]))
    g = _bdot(_sigmoid(xg), g2_ref[...])

    kk = k * kk_ref[...]
    kk = kk / jnp.maximum(jnp.sqrt(_head_sum(kk * kk)), L2_EPS)
    k_mod = k * (1.0 + (al - 1.0) * ka_ref[...])
    b_vec = kk * al
    bonus = _head_sum(r * k_mod * rk_ref[...]) * v

    tri = _tri(CHUNK, strict=False).astype(jnp.float32)
    for c in range(nchunk):
        sl = slice(c * CHUNK, (c + 1) * CHUNK)
        lwc = lw[sl]
        cl = jnp.dot(tri, lwc, precision=HI, preferred_element_type=jnp.float32)
        el = jnp.exp(cl)
        inv = jnp.exp(-cl)
        lhs_ref[c, 0:CHUNK, :] = -kk[sl] * jnp.exp(cl - lwc)
        lhs_ref[c, CHUNK:2 * CHUNK, :] = r[sl] * el
        rhs_ref[c, 0:CHUNK, :] = b_vec[sl] * inv
        rhs_ref[c, CHUNK:2 * CHUNK, :] = k_mod[sl] * inv
        v_ref[c] = v[sl]
        wc_ref[c] = el[CHUNK - 1:CHUNK, :]

    r2 = lax.broadcasted_iota(jnp.int32, (2 * CHUNK, 2 * CHUNK), 0)
    c2 = lax.broadcasted_iota(jnp.int32, (2 * CHUNK, 2 * CHUNK), 1)
    tt = r2 & (CHUNK - 1)
    jj = c2 & (CHUNK - 1)
    aa_mask = (tt > jj) | ((r2 >= CHUNK) & (tt == jj))
    eye = (_tri(CHUNK, False) & ~_tri(CHUNK, True)).astype(jnp.float32)
    ri = lax.broadcasted_iota(jnp.int32, (CHUNK, CHUNK), 0)
    ci = lax.broadcasted_iota(jnp.int32, (CHUNK, CHUNK), 1)
    lvl_masks = [((ri >> (l + 1)) == (ci >> (l + 1))) & ((ri >> l) != (ci >> l))
                 for l in range(CHUNK.bit_length() - 1)]

    def chunk_body(c, _):
        for h in range(HEADS):
            hs = slice(h * HEAD_DIM, (h + 1) * HEAD_DIM)
            lhs = lhs_ref[c, :, hs]
            rhs = rhs_ref[c, :, hs]
            vh = v_ref[c, :, hs]
            st = st_ref[h]
            aa = lax.dot_general(lhs.astype(jnp.bfloat16), rhs.astype(jnp.bfloat16), NT,
                                 preferred_element_type=jnp.float32)
            aa = jnp.where(aa_mask, aa, 0.0)
            a_ab = aa[0:CHUNK, 0:CHUNK]
            a_ak = aa[0:CHUNK, CHUNK:]
            a_r = aa[CHUNK:, :]
            t = eye + jnp.where(lvl_masks[0], a_ab, 0.0)
            for lm in lvl_masks[1:]:
                t = t + _bdot(t, _bdot(jnp.where(lm, a_ab, 0.0), t))
            p0 = lax.dot_general(lhs.astype(jnp.bfloat16), st.astype(jnp.bfloat16), NT,
                                 preferred_element_type=jnp.float32)
            x = p0[0:CHUNK] + _bdot(a_ak, vh)
            u = _bdot(t, x)
            uv = jnp.concatenate([u, vh], axis=0)
            y = p0[CHUNK:] + _bdot(a_r, uv)
            y_ref[c, :, hs] = y
            upd = lax.dot_general(uv.astype(jnp.bfloat16), rhs.astype(jnp.bfloat16), TN,
                                  preferred_element_type=jnp.float32)
            st_ref[h] = (st + upd) * wc_ref[c, :, hs]
        return 0

    lax.fori_loop(0, nchunk, chunk_body, 0)

    yv = y_ref[...].reshape(tm, WIDTH)
    mean = _head_sum(yv) * (1.0 / HEAD_DIM)
    d = yv - mean
    var = _head_sum(d * d) * (1.0 / HEAD_DIM)
    yn = d * lax.rsqrt(var + GN_EPS) * lnw_ref[...] + lnb_ref[...]
    o_ref[0] = ((yn + bonus) * g).astype(o_ref.dtype)


def _rwkv_call(p, mu_p, w0, w2_p, a0, a2_p, g2, k_k, k_a, r_k, ln_w, ln_b, tm):
    bsz, seq, _ = p.shape
    nchunk = tm // CHUNK
    row = lambda n: pl.BlockSpec((1, n), lambda b, s: (0, 0))
    mat = lambda m, n: pl.BlockSpec((m, n), lambda b, s: (0, 0))
    return pl.pallas_call(
        _rwkv_kernel,
        grid=(bsz, seq // tm),
        in_specs=[pl.BlockSpec((1, tm, RWKV_P), lambda b, s: (b, s, 0)),
                  row(RWKV_P), row(WIDTH), mat(LORA_PAD, WIDTH), row(WIDTH), mat(LORA_PAD, WIDTH),
                  mat(GATE_LORA, WIDTH), row(WIDTH), row(WIDTH), row(WIDTH), row(WIDTH), row(WIDTH)],
        out_specs=pl.BlockSpec((1, tm, WIDTH), lambda b, s: (b, s, 0)),
        out_shape=jax.ShapeDtypeStruct((bsz, seq, WIDTH), jnp.bfloat16),
        scratch_shapes=[pltpu.VMEM((HEADS, HEAD_DIM, HEAD_DIM), jnp.float32),
                        pltpu.VMEM((1, RWKV_P), jnp.float32),
                        pltpu.VMEM((nchunk, 2 * CHUNK, WIDTH), jnp.float32),
                        pltpu.VMEM((nchunk, 2 * CHUNK, WIDTH), jnp.float32),
                        pltpu.VMEM((nchunk, CHUNK, WIDTH), jnp.float32),
                        pltpu.VMEM((nchunk, 1, WIDTH), jnp.float32),
                        pltpu.VMEM((nchunk, CHUNK, WIDTH), jnp.float32)],
        compiler_params=pltpu.CompilerParams(
            dimension_semantics=("arbitrary", "arbitrary"), vmem_limit_bytes=VMEM_LIMIT),
        name="rwkv7",
    )(p, mu_p, w0, w2_p, a0, a2_p, g2, k_k, k_a, r_k, ln_w, ln_b)


FF_CHUNK = 256


def _post_kernel(of_ref, or_ref, gate_ref, x_ref, mod_ref, wfo_ref, wro_ref, wo_ref, g2_ref,
                 wup_ref, cw_ref, cb_ref, wdn_ref, fg_ref, o_ref, carry_ref):
    s = pl.program_id(1)
    tm = x_ref.shape[1]

    @pl.when(s == 0)
    def _():
        carry_ref[...] = jnp.zeros_like(carry_ref)

    mod = mod_ref[0]
    gate1 = mod[:, 2 * D_MODEL:3 * D_MODEL]
    shift2 = mod[:, 3 * D_MODEL:4 * D_MODEL]
    scale2 = mod[:, 4 * D_MODEL:5 * D_MODEL]
    gate2 = mod[:, 5 * D_MODEL:6 * D_MODEL]

    gate = gate_ref[0]
    m_fox = jnp.dot(of_ref[0], wfo_ref[...], preferred_element_type=jnp.float32)
    m_rwkv = jnp.dot(or_ref[0], wro_ref[...], preferred_element_type=jnp.float32)
    merged = gate[:, 0:D_MODEL] * m_fox + gate[:, D_MODEL:] * m_rwkv
    mix = jnp.dot(merged.astype(jnp.bfloat16), wo_ref[...], preferred_element_type=jnp.float32)
    h1 = x_ref[0] + gate1 * mix

    y = h1 * lax.rsqrt(jnp.mean(h1 * h1, axis=-1, keepdims=True) + RMS_EPS) * g2_ref[...]
    u2 = (y * (1.0 + scale2) + shift2).astype(jnp.bfloat16)

    row = lax.broadcasted_iota(jnp.int32, (tm, FF_CHUNK), 0)
    acc = jnp.zeros((tm, D_MODEL), jnp.float32)
    for c in range(D_FF // FF_CHUNK):
        cs = slice(c * FF_CHUNK, (c + 1) * FF_CHUNK)
        a = jnp.dot(u2, wup_ref[:, cs], preferred_element_type=jnp.float32)
        b = jnp.dot(u2, wup_ref[:, D_FF + c * FF_CHUNK:D_FF + (c + 1) * FF_CHUNK],
                    preferred_element_type=jnp.float32)
        tail = carry_ref[:, cs]
        p1 = tail[7:8, :]
        p2 = tail[6:7, :]
        a1 = jnp.where(row == 0, p1, pltpu.roll(a, 1, 0))
        a2 = jnp.where(row == 0, p2, jnp.where(row == 1, p1, pltpu.roll(a, 2, 0)))
        carry_ref[:, cs] = a[tm - 8:tm, :]
        cw = cw_ref[:, cs]
        conv = cw[0:1, :] * a2 + cw[1:2, :] * a1 + cw[2:3, :] * a + cb_ref[:, cs]
        act = conv * _sigmoid(conv) * b
        acc = acc + jnp.dot(act.astype(jnp.bfloat16), wdn_ref[cs, :], preferred_element_type=jnp.float32)

    h2 = h1 + gate2 * acc
    o_ref[0] = h2 * lax.rsqrt(jnp.mean(h2 * h2, axis=-1, keepdims=True) + RMS_EPS) * fg_ref[...]


def _post_call(o_fox, o_rwkv, gates, x, mod3, wfo, wro, wo, ln2_g, wup, conv_w, conv_b, wdn, final_g, tm):
    bsz, seq, _ = x.shape
    tok = lambda n: pl.BlockSpec((1, tm, n), lambda b, s: (b, s, 0))
    const = lambda shape: pl.BlockSpec(shape, lambda b, s: (0,) * len(shape),
                                       pipeline_mode=pl.Buffered(1))
    return pl.pallas_call(
        _post_kernel,
        grid=(bsz, seq // tm),
        in_specs=[tok(WIDTH), tok(WIDTH), tok(2 * D_MODEL), tok(D_MODEL),
                  pl.BlockSpec((1, 1, 6 * D_MODEL), lambda b, s: (b, 0, 0)),
                  const((WIDTH, D_MODEL)), const((WIDTH, D_MODEL)), const((D_MODEL, D_MODEL)),
                  const((1, D_MODEL)), const((D_MODEL, 2 * D_FF)), const((3, D_FF)), const((1, D_FF)),
                  const((D_FF, D_MODEL)), const((1, D_MODEL))],
        out_specs=tok(D_MODEL),
        out_shape=jax.ShapeDtypeStruct((bsz, seq, D_MODEL), jnp.float32),
        scratch_shapes=[pltpu.VMEM((8, D_FF), jnp.float32)],
        compiler_params=pltpu.CompilerParams(
            dimension_semantics=("arbitrary", "arbitrary"), vmem_limit_bytes=VMEM_LIMIT),
        name="post_ffn",
    )(o_fox, o_rwkv, gates, x, mod3, wfo, wro, wo, ln2_g, wup, conv_w, conv_b, wdn, final_g)


def _pad_cols(w, n):
    return jnp.pad(w, ((0, 0), (0, n - w.shape[1])))


def _layer(h, c, w_mod, b_mod, ln1_g, w_in, fox_b_f, fox_w_out, rwkv_mu, rwkv_w0, rwkv_w2, rwkv_a0,
           rwkv_a2, rwkv_g2, rwkv_k_k, rwkv_k_a, rwkv_r_k, rwkv_ln_w, rwkv_ln_b, rwkv_w_out, w_o,
           ln2_g, w_up, conv_w, conv_b, w_down, out_gain):
    bsz, seq, _ = h.shape
    bf16 = jnp.bfloat16
    lora = (LORA_PAD - 64)

    fox_cols = 3 * WIDTH + HEADS
    o_r = fox_cols
    o_g = o_r + 3 * WIDTH + 64 + 64 + GATE_LORA
    w_in_p = jnp.concatenate([
        w_in[:, 0:3 * WIDTH], _pad_cols(w_in[:, 3 * WIDTH:fox_cols], LANES),
        w_in[:, o_r:o_r + 3 * WIDTH],
        _pad_cols(w_in[:, o_r + 3 * WIDTH:o_r + 3 * WIDTH + 64], LORA_PAD),
        _pad_cols(w_in[:, o_r + 3 * WIDTH + 64:o_r + 3 * WIDTH + 128], LORA_PAD),
        w_in[:, o_r + 3 * WIDTH + 128:o_g], w_in[:, o_g:]], axis=1).astype(bf16)
    mu = rwkv_mu.reshape(1, -1)
    mu_p = jnp.concatenate([mu[:, 0:3 * WIDTH], _pad_cols(mu[:, 3 * WIDTH:3 * WIDTH + 64], LORA_PAD),
                            _pad_cols(mu[:, 3 * WIDTH + 64:3 * WIDTH + 128], LORA_PAD),
                            mu[:, 3 * WIDTH + 128:]], axis=1)
    w2_p = jnp.pad(rwkv_w2, ((0, lora), (0, 0))).astype(bf16)
    a2_p = jnp.pad(rwkv_a2, ((0, lora), (0, 0))).astype(bf16)
    bf_pad = _pad_cols(fox_b_f.reshape(1, HEADS), LANES)
    rowv = lambda t: t.reshape(1, -1)

    tm_in = min(512, seq)
    tq = min(256, seq)
    tm_rw = min(256, seq)
    tm_post = min(512, seq)

    mod = _mod_call(c, w_mod, b_mod)
    mod3 = mod.reshape(bsz, 1, 6 * D_MODEL)
    q, k, v, cum, cumt, p_rwkv, gates = _inproj_call(h, mod3, rowv(ln1_g), bf_pad, w_in_p, tm_in)
    o_fox = _fox_call(q, k, v, cum, cumt, tq)
    o_rwkv = _rwkv_call(p_rwkv, mu_p, rowv(rwkv_w0), w2_p, rowv(rwkv_a0), a2_p, rwkv_g2.astype(bf16),
                        rowv(rwkv_k_k), rowv(rwkv_k_a), rowv(rwkv_r_k), rowv(rwkv_ln_w),
                        rowv(rwkv_ln_b), tm_rw)
    return _post_call(o_fox, o_rwkv, gates, h, mod3, fox_w_out.astype(bf16), rwkv_w_out.astype(bf16),
                      w_o.astype(bf16), rowv(ln2_g), w_up.astype(bf16), conv_w, rowv(conv_b),
                      w_down.astype(bf16), rowv(out_gain), tm_post)


def kernel(x, c, w_mod, b_mod, ln1_g, w_in, fox_b_f, fox_w_out, rwkv_mu, rwkv_w0, rwkv_w2, rwkv_a0,
           rwkv_a2, rwkv_g2, rwkv_k_k, rwkv_k_a, rwkv_r_k, rwkv_ln_w, rwkv_ln_b, rwkv_w_out, w_o,
           ln2_g, w_up, conv_w, conv_b, w_down, final_g):
    depth = w_mod.shape[0]
    assert depth == 1, "single trunk layer: the final RMSNorm is fused into the layer's last kernel"
    return _layer(x, c, w_mod[0], b_mod[0], ln1_g[0], w_in[0], fox_b_f[0], fox_w_out[0], rwkv_mu[0],
                  rwkv_w0[0], rwkv_w2[0], rwkv_a0[0], rwkv_a2[0], rwkv_g2[0], rwkv_k_k[0], rwkv_k_a[0],
                  rwkv_r_k[0], rwkv_ln_w[0], rwkv_ln_b[0], rwkv_w_out[0], w_o[0], ln2_g[0], w_up[0],
                  conv_w[0], conv_b[0], w_down[0], final_g)
```

```python
import functools

import jax
import jax.numpy as jnp
from jax import lax
from jax.experimental import pallas as pl
from jax.experimental.pallas import tpu as pltpu

D_MODEL = 1024
HEADS = 8
HEAD_DIM = 64
WIDTH = HEADS * HEAD_DIM
LORA_PAD = 128
GATE_LORA = 128
D_FF = 2816
RMS_EPS = 1e-6
GN_EPS = HEAD_DIM * 1e-5
L2_EPS = 1e-12
LANES = 128
CHUNK = 64

OFF_QKV = 0
OFF_F = 3 * WIDTH
OFF_RWKV = OFF_F + LANES
RWKV_P = 3 * WIDTH + 2 * LORA_PAD + GATE_LORA
OFF_GATE = OFF_RWKV + RWKV_P
IN_P = OFF_GATE + 2 * D_MODEL

VMEM_LIMIT = 56 * 1024 * 1024

HI = lax.Precision.HIGHEST
NT = (((1,), (1,)), ((), ()))


def _bdot(a, b):
    return jnp.dot(a.astype(jnp.bfloat16), b.astype(jnp.bfloat16), preferred_element_type=jnp.float32)


def _sigmoid(x):
    return 1.0 / (1.0 + jnp.exp(-x))


def _softplus(x):
    return jnp.maximum(x, 0.0) + jnp.log(1.0 + jnp.exp(-jnp.abs(x)))


def _tri(n, strict):
    r = lax.broadcasted_iota(jnp.int32, (n, n), 0)
    c = lax.broadcasted_iota(jnp.int32, (n, n), 1)
    return (r > c) if strict else (r >= c)


def _head_sum(x):
    lane = lax.broadcasted_iota(jnp.int32, (x.shape[0], LANES), 1)
    lo = lane < HEAD_DIM
    outs = []
    for g in range(x.shape[1] // LANES):
        xg = x[:, g * LANES:(g + 1) * LANES]
        s0 = jnp.sum(jnp.where(lo, xg, 0.0), axis=-1, keepdims=True)
        s1 = jnp.sum(jnp.where(lo, 0.0, xg), axis=-1, keepdims=True)
        outs.append(jnp.where(lo, s0, s1))
    return jnp.concatenate(outs, axis=-1)


def _mod_kernel(c_ref, w_ref, b_ref, o_ref):
    c = c_ref[...]
    sc = c * _sigmoid(c)
    o_ref[...] = jnp.dot(sc, w_ref[...], precision=HI, preferred_element_type=jnp.float32) + b_ref[...]


def _mod_call(c, w_mod, b_mod):
    bsz = c.shape[0]
    n = w_mod.shape[1]
    tn = 1024
    return pl.pallas_call(
        _mod_kernel,
        grid=(n // tn,),
        in_specs=[pl.BlockSpec((bsz, D_MODEL), lambda j: (0, 0)),
                  pl.BlockSpec((D_MODEL, tn), lambda j: (0, j)),
                  pl.BlockSpec((1, tn), lambda j: (0, j))],
        out_specs=pl.BlockSpec((bsz, tn), lambda j: (0, j)),
        out_shape=jax.ShapeDtypeStruct((bsz, n), jnp.float32),
        name="mod",
    )(c, w_mod, b_mod.reshape(1, n))


def _split3(c):
    f32 = jnp.float32
    hi = c.astype(jnp.bfloat16).astype(f32)
    mid = (c - hi).astype(jnp.bfloat16).astype(f32)
    lo = (c - hi - mid).astype(jnp.bfloat16).astype(f32)
    return hi, mid, lo


def _cumsum_rows(tri, x):
    return sum(jnp.dot(tri, part.astype(jnp.bfloat16), preferred_element_type=jnp.float32)
               for part in _split3(x))


def _inproj_kernel(x_ref, mod_ref, g_ref, bf_ref, w_ref,
                   q_ref, k_ref, v_ref, p_ref, gate_ref, carry_ref):
    s = pl.program_id(1)
    tm = x_ref.shape[1]

    @pl.when(s == 0)
    def _():
        carry_ref[...] = jnp.zeros_like(carry_ref)

    x = x_ref[0]
    mod = mod_ref[0]
    shift = mod[:, 0:D_MODEL]
    scale = mod[:, D_MODEL:2 * D_MODEL]
    y = x * lax.rsqrt(jnp.mean(x * x, axis=-1, keepdims=True) + RMS_EPS) * g_ref[...]
    u = (y * (1.0 + scale) + shift).astype(jnp.bfloat16)

    def proj(off, n):
        return jnp.dot(u, w_ref[:, off:off + n], preferred_element_type=jnp.float32)

    z = proj(OFF_F, LANES) + bf_ref[...]
    logf = jnp.minimum(z, 0.0) - jnp.log(1.0 + jnp.exp(-jnp.abs(z)))
    cum = _cumsum_rows(_tri(tm, strict=False).astype(jnp.bfloat16), logf) + carry_ref[...]
    carry_ref[...] = cum[tm - 1:tm, :]

    q = proj(OFF_QKV, WIDTH) * (HEAD_DIM ** -0.5)
    k = proj(OFF_QKV + WIDTH, WIDTH)
    v = proj(OFF_QKV + 2 * WIDTH, WIDTH)
    lane = lax.broadcasted_iota(jnp.int32, (tm, LANES), 1)
    lo = lane < HEAD_DIM
    for h in range(HEADS):
        g = slice((h // 2) * LANES, (h // 2 + 1) * LANES)
        take = (lambda t: t[:, g]) if h % 2 == 0 else (lambda t: pltpu.roll(t[:, g], HEAD_DIM, 1))
        c_hi, c_mid, c_lo = _split3(cum[:, h:h + 1])
        parts = jnp.where(lane == HEAD_DIM, c_hi, jnp.where(lane == HEAD_DIM + 1, c_mid, c_lo))
        q_bias = jnp.where(lane < HEAD_DIM + 3, parts, jnp.where(lane < HEAD_DIM + 6, -1.0, 0.0))
        parts = jnp.where(lane == HEAD_DIM + 3, c_hi, jnp.where(lane == HEAD_DIM + 4, c_mid, c_lo))
        k_bias = jnp.where(lane < HEAD_DIM + 3, 1.0, jnp.where(lane < HEAD_DIM + 6, parts, 0.0))
        v_one = jnp.where(lane == HEAD_DIM, 1.0, 0.0)
        q_ref[0, h] = jnp.where(lo, take(q), q_bias).astype(jnp.bfloat16)
        k_ref[0, h] = jnp.where(lo, take(k), k_bias).astype(jnp.bfloat16)
        v_ref[0, h] = jnp.where(lo, take(v), v_one).astype(jnp.bfloat16)

    for off in range(0, RWKV_P, 384):
        p_ref[0, :, off:off + 384] = proj(OFF_RWKV + off, 384)
    for off in range(0, 2 * D_MODEL, 512):
        gate_ref[0, :, off:off + 512] = _sigmoid(proj(OFF_GATE + off, 512)).astype(jnp.bfloat16)


def _inproj_call(x, mod3, ln1_g, bf_pad, w_in_p, tm):
    bsz, seq, _ = x.shape
    grid = (bsz, seq // tm)
    tok = lambda n: pl.BlockSpec((1, tm, n), lambda b, s: (b, s, 0))
    const = lambda shape: pl.BlockSpec(shape, lambda b, s: (0,) * len(shape))
    head = pl.BlockSpec((1, HEADS, tm, LANES), lambda b, s: (b, 0, s, 0))
    out_shapes = (
        jax.ShapeDtypeStruct((bsz, HEADS, seq, LANES), jnp.bfloat16),
        jax.ShapeDtypeStruct((bsz, HEADS, seq, LANES), jnp.bfloat16),
        jax.ShapeDtypeStruct((bsz, HEADS, seq, LANES), jnp.bfloat16),
        jax.ShapeDtypeStruct((bsz, seq, RWKV_P), jnp.float32),
        jax.ShapeDtypeStruct((bsz, seq, 2 * D_MODEL), jnp.bfloat16),
    )
    return pl.pallas_call(
        _inproj_kernel,
        grid=grid,
        in_specs=[tok(D_MODEL),
                  pl.BlockSpec((1, 1, 6 * D_MODEL), lambda b, s: (b, 0, 0)),
                  const((1, D_MODEL)), const((1, LANES)), const((D_MODEL, IN_P))],
        out_specs=(head, head, head, tok(RWKV_P), tok(2 * D_MODEL)),
        out_shape=out_shapes,
        scratch_shapes=[pltpu.VMEM((1, LANES), jnp.float32)],
        compiler_params=pltpu.CompilerParams(
            dimension_semantics=("arbitrary", "arbitrary"), vmem_limit_bytes=VMEM_LIMIT),
        name="in_proj",
    )(x, mod3, ln1_g, bf_pad, w_in_p)


NEG = -1e30


def _fox_kernel(q_ref, k_ref, v_ref, o_ref, *, tq):
    i = pl.program_id(2)
    causal = _tri(tq, strict=False)
    qs = [q_ref[0, hh] for hh in range(2)]

    def block(j, carry, masked):
        j0 = pl.multiple_of(j * tq, tq)
        new = []
        for hh in range(2):
            m, acc = carry[hh]
            kb = k_ref[0, hh, pl.ds(j0, tq), :]
            vb = v_ref[0, hh, pl.ds(j0, tq), :]
            sc = lax.dot_general(qs[hh], kb, NT, preferred_element_type=jnp.float32)
            if masked:
                sc = jnp.where(causal, sc, NEG)
            m_new = jnp.maximum(m, jnp.max(sc, axis=-1, keepdims=True))
            p = jnp.exp(sc - m_new).astype(jnp.bfloat16)
            acc = jnp.exp(m - m_new) * acc + jnp.dot(p, vb, preferred_element_type=jnp.float32)
            new.append((m_new, acc))
        return tuple(new)

    init = tuple((jnp.full((tq, 1), NEG, jnp.float32), jnp.zeros((tq, LANES), jnp.float32))
                 for _ in range(2))
    carry = lax.fori_loop(0, i, functools.partial(block, masked=False), init)
    (_, acc0), (_, acc1) = block(i, carry, masked=True)
    o0 = acc0 / acc0[:, HEAD_DIM:HEAD_DIM + 1]
    o1 = pltpu.roll(acc1 / acc1[:, HEAD_DIM:HEAD_DIM + 1], HEAD_DIM, 1)
    lane = lax.broadcasted_iota(jnp.int32, (tq, LANES), 1)
    o_ref[0] = jnp.where(lane < HEAD_DIM, o0, o1).astype(o_ref.dtype)


def _fox_call(q_aug, k_aug, v_aug, tq):
    bsz, _, seq, _ = q_aug.shape
    npair = HEADS // 2
    kv_spec = pl.BlockSpec((1, 2, seq, LANES), lambda b, p, i: (b, p, 0, 0))
    return pl.pallas_call(
        functools.partial(_fox_kernel, tq=tq),
        grid=(bsz, npair, seq // tq),
        in_specs=[pl.BlockSpec((1, 2, tq, LANES), lambda b, p, i: (b, p, i, 0)), kv_spec, kv_spec],
        out_specs=pl.BlockSpec((1, tq, LANES), lambda b, p, i: (b, i, p)),
        out_shape=jax.ShapeDtypeStruct((bsz, seq, WIDTH), jnp.bfloat16),
        compiler_params=pltpu.CompilerParams(
            dimension_semantics=("arbitrary", "arbitrary", "arbitrary"), vmem_limit_bytes=VMEM_LIMIT),
        name="fox_attn",
    )(q_aug, k_aug, v_aug)


BMM = (((2,), (1,)), ((0,), (0,)))


def _bmm(a, b):
    return lax.dot_general(a.astype(jnp.bfloat16), b.astype(jnp.bfloat16), BMM,
                           preferred_element_type=jnp.float32)


def _rwkv_kernel(p_ref, mu_ref, w0_ref, w2_ref, a0_ref, a2_ref, g2_ref, kk_ref, ka_ref, rk_ref,
                 lnw_ref, lnb_ref, o_ref,
                 st_ref, carry_ref, lhs_ref, rhst_ref, suv_ref, wc_ref, ta_ref, tx_ref, yl_ref, y_ref):
    s = pl.program_id(1)
    tm = p_ref.shape[1]
    nchunk = tm // CHUNK

    @pl.when(s == 0)
    def _():
        carry_ref[...] = jnp.zeros_like(carry_ref)
        st_ref[...] = jnp.zeros_like(st_ref)

    p = p_ref[0]
    row = lax.broadcasted_iota(jnp.int32, p.shape, 0)
    prev = jnp.where(row == 0, carry_ref[...], pltpu.roll(p, 1, 0))
    carry_ref[...] = p[tm - 1:tm, :]
    xl = p + (prev - p) * mu_ref[...]
    r = xl[:, 0:WIDTH]
    k = xl[:, WIDTH:2 * WIDTH]
    v = xl[:, 2 * WIDTH:3 * WIDTH]
    xw = xl[:, 3 * WIDTH:3 * WIDTH + LORA_PAD]
    xa = xl[:, 3 * WIDTH + LORA_PAD:3 * WIDTH + 2 * LORA_PAD]
    xg = xl[:, 3 * WIDTH + 2 * LORA_PAD:RWKV_P]

    zw = w0_ref[...] + _bdot(jnp.tanh(xw), w2_ref[...])
    lw = -jnp.exp(-_softplus(-zw) - 0.5)
    al = _sigmoid(a0_ref[...] + _bdot(xa, a2_ref[...]))
    g = _bdot(_sigmoid(xg), g2_ref[...])

    kk = k * kk_ref[...]
    kk = kk / jnp.maximum(jnp.sqrt(_head_sum(kk * kk)), L2_EPS)
    k_mod = k * (1.0 + (al - 1.0) * ka_ref[...])
    b_vec = kk * al
    bonus = _head_sum(r * k_mod * rk_ref[...]) * v

    tri = _tri(CHUNK, strict=False).astype(jnp.bfloat16)
    for c in range(nchunk):
        sl = slice(c * CHUNK, (c + 1) * CHUNK)
        lwc = lw[sl]
        cl = _cumsum_rows(tri, lwc)
        el = jnp.exp(cl)
        inv = jnp.exp(-cl)
        lhs = jnp.concatenate([-kk[sl] * jnp.exp(cl - lwc), r[sl] * el], axis=0)
        rhst = jnp.concatenate([b_vec[sl] * inv, k_mod[sl] * inv], axis=0).T
        wct = el[CHUNK - 8:CHUNK, :].T[:, 7:8]
        vc = v[sl]
        for h in range(HEADS):
            i = c * HEADS + h
            hs = slice(h * HEAD_DIM, (h + 1) * HEAD_DIM)
            lhs_ref[i] = lhs[:, hs].astype(jnp.bfloat16)
            rhst_ref[i] = rhst[hs, :].astype(jnp.bfloat16)
            suv_ref[i, 2 * CHUNK:3 * CHUNK, :] = vc[:, hs].astype(jnp.bfloat16)
            wc_ref[i] = jnp.broadcast_to(wct[hs, :], (HEAD_DIM, HEAD_DIM))

    r2 = lax.broadcasted_iota(jnp.int32, (2 * CHUNK, 2 * CHUNK), 0)
    c2 = lax.broadcasted_iota(jnp.int32, (2 * CHUNK, 2 * CHUNK), 1)
    tt = r2 & (CHUNK - 1)
    jj = c2 & (CHUNK - 1)
    aa_mask = (tt > jj) | ((r2 >= CHUNK) & (tt == jj))
    eye = (_tri(CHUNK, False) & ~_tri(CHUNK, True)).astype(jnp.float32)
    ri = lax.broadcasted_iota(jnp.int32, (CHUNK, CHUNK), 0)
    ci = lax.broadcasted_iota(jnp.int32, (CHUNK, CHUNK), 1)
    lvl_masks = [((ri >> (l + 1)) == (ci >> (l + 1))) & ((ri >> l) != (ci >> l))
                 for l in range(CHUNK.bit_length() - 1)]

    lhs = lhs_ref[...]
    aa = jnp.where(aa_mask, _bmm(lhs, rhst_ref[...]), 0.0)
    a_ab = aa[:, 0:CHUNK, 0:CHUNK]
    a_ak = aa[:, 0:CHUNK, CHUNK:]
    a_r = aa[:, CHUNK:, :]
    t = eye + jnp.where(lvl_masks[0], a_ab, 0.0)
    for lm in lvl_masks[1:]:
        t = t + _bmm(t, _bmm(jnp.where(lm, a_ab, 0.0), t))
    x0 = _bmm(a_ak, suv_ref[:, 2 * CHUNK:3 * CHUNK, :])
    tt = _bmm(t, jnp.concatenate([lhs[:, 0:CHUNK, :], x0.astype(jnp.bfloat16)], axis=-1))
    ta_ref[...] = tt[:, :, 0:HEAD_DIM].astype(jnp.bfloat16)
    tx_ref[...] = tt[:, :, HEAD_DIM:]
    yl_ref[...] = jnp.concatenate([lhs[:, CHUNK:, :], a_r.astype(jnp.bfloat16)], axis=-1)

    def chunk_body(c, _):
        i0 = pl.multiple_of(c * HEADS, HEADS)
        blk = pl.ds(i0, HEADS)
        st = st_ref[...]
        stb = st.astype(jnp.bfloat16)
        u = (_bmm(ta_ref[blk], stb) + tx_ref[blk]).astype(jnp.bfloat16)
        suv_ref[blk, 0:CHUNK, :] = stb
        suv_ref[blk, CHUNK:2 * CHUNK, :] = u
        uv = jnp.concatenate([u, suv_ref[blk, 2 * CHUNK:3 * CHUNK, :]], axis=1)
        st_ref[...] = wc_ref[blk] * (st + _bmm(rhst_ref[blk], uv))
        return 0

    lax.fori_loop(0, nchunk, chunk_body, 0)

    y = _bmm(yl_ref[...], suv_ref[...])
    for c in range(nchunk):
        for h in range(HEADS):
            y_ref[c, :, h * HEAD_DIM:(h + 1) * HEAD_DIM] = y[c * HEADS + h]

    yv = y_ref[...].reshape(tm, WIDTH)
    mean = _head_sum(yv) * (1.0 / HEAD_DIM)
    d = yv - mean
    var = _head_sum(d * d) * (1.0 / HEAD_DIM)
    yn = d * lax.rsqrt(var + GN_EPS) * lnw_ref[...] + lnb_ref[...]
    o_ref[0] = ((yn + bonus) * g).astype(o_ref.dtype)


def _rwkv_call(p, mu_p, w0, w2_p, a0, a2_p, g2, k_k, k_a, r_k, ln_w, ln_b, tm):
    bsz, seq, _ = p.shape
    nchunk = tm // CHUNK
    nb = nchunk * HEADS
    row = lambda n: pl.BlockSpec((1, n), lambda b, s: (0, 0))
    mat = lambda m, n: pl.BlockSpec((m, n), lambda b, s: (0, 0))
    return pl.pallas_call(
        _rwkv_kernel,
        grid=(bsz, seq // tm),
        in_specs=[pl.BlockSpec((1, tm, RWKV_P), lambda b, s: (b, s, 0)),
                  row(RWKV_P), row(WIDTH), mat(LORA_PAD, WIDTH), row(WIDTH), mat(LORA_PAD, WIDTH),
                  mat(GATE_LORA, WIDTH), row(WIDTH), row(WIDTH), row(WIDTH), row(WIDTH), row(WIDTH)],
        out_specs=pl.BlockSpec((1, tm, WIDTH), lambda b, s: (b, s, 0)),
        out_shape=jax.ShapeDtypeStruct((bsz, seq, WIDTH), jnp.bfloat16),
        scratch_shapes=[pltpu.VMEM((HEADS, HEAD_DIM, HEAD_DIM), jnp.float32),
                        pltpu.VMEM((1, RWKV_P), jnp.float32),
                        pltpu.VMEM((nb, 2 * CHUNK, HEAD_DIM), jnp.bfloat16),
                        pltpu.VMEM((nb, HEAD_DIM, 2 * CHUNK), jnp.bfloat16),
                        pltpu.VMEM((nb, 3 * CHUNK, HEAD_DIM), jnp.bfloat16),
                        pltpu.VMEM((nb, HEAD_DIM, HEAD_DIM), jnp.float32),
                        pltpu.VMEM((nb, CHUNK, HEAD_DIM), jnp.bfloat16),
                        pltpu.VMEM((nb, CHUNK, HEAD_DIM), jnp.float32),
                        pltpu.VMEM((nb, CHUNK, 3 * CHUNK), jnp.bfloat16),
                        pltpu.VMEM((nchunk, CHUNK, WIDTH), jnp.float32)],
        compiler_params=pltpu.CompilerParams(
            dimension_semantics=("arbitrary", "arbitrary"), vmem_limit_bytes=VMEM_LIMIT),
        name="rwkv7",
    )(p, mu_p, w0, w2_p, a0, a2_p, g2, k_k, k_a, r_k, ln_w, ln_b)


FF_CHUNK = 256


def _post_kernel(of_ref, or_ref, gate_ref, x_ref, mod_ref, wfo_ref, wro_ref, wo_ref, g2_ref,
                 wup_ref, cw_ref, cb_ref, wdn_ref, fg_ref, o_ref, carry_ref):
    s = pl.program_id(1)
    tm = x_ref.shape[1]

    @pl.when(s == 0)
    def _():
        carry_ref[...] = jnp.zeros_like(carry_ref)

    mod = mod_ref[0]
    gate1 = mod[:, 2 * D_MODEL:3 * D_MODEL]
    shift2 = mod[:, 3 * D_MODEL:4 * D_MODEL]
    scale2 = mod[:, 4 * D_MODEL:5 * D_MODEL]
    gate2 = mod[:, 5 * D_MODEL:6 * D_MODEL]

    gate = gate_ref[0]
    m_fox = jnp.dot(of_ref[0], wfo_ref[...], preferred_element_type=jnp.float32)
    m_rwkv = jnp.dot(or_ref[0], wro_ref[...], preferred_element_type=jnp.float32)
    merged = gate[:, 0:D_MODEL] * m_fox + gate[:, D_MODEL:] * m_rwkv
    mix = jnp.dot(merged.astype(jnp.bfloat16), wo_ref[...], preferred_element_type=jnp.float32)
    h1 = x_ref[0] + gate1 * mix

    y = h1 * lax.rsqrt(jnp.mean(h1 * h1, axis=-1, keepdims=True) + RMS_EPS) * g2_ref[...]
    u2 = (y * (1.0 + scale2) + shift2).astype(jnp.bfloat16)

    row = lax.broadcasted_iota(jnp.int32, (tm, FF_CHUNK), 0)
    acc = jnp.zeros((tm, D_MODEL), jnp.float32)
    for c in range(D_FF // FF_CHUNK):
        cs = slice(c * FF_CHUNK, (c + 1) * FF_CHUNK)
        a = jnp.dot(u2, wup_ref[:, cs], preferred_element_type=jnp.float32)
        b = jnp.dot(u2, wup_ref[:, D_FF + c * FF_CHUNK:D_FF + (c + 1) * FF_CHUNK],
                    preferred_element_type=jnp.float32)
        tail = carry_ref[:, cs]
        p1 = tail[7:8, :]
        p2 = tail[6:7, :]
        a1 = jnp.where(row == 0, p1, pltpu.roll(a, 1, 0))
        a2 = jnp.where(row == 0, p2, jnp.where(row == 1, p1, pltpu.roll(a, 2, 0)))
        carry_ref[:, cs] = a[tm - 8:tm, :]
        cw = cw_ref[:, cs]
        conv = cw[0:1, :] * a2 + cw[1:2, :] * a1 + cw[2:3, :] * a + cb_ref[:, cs]
        act = conv * _sigmoid(conv) * b
        acc = acc + jnp.dot(act.astype(jnp.bfloat16), wdn_ref[cs, :], preferred_element_type=jnp.float32)

    h2 = h1 + gate2 * acc
    o_ref[0] = h2 * lax.rsqrt(jnp.mean(h2 * h2, axis=-1, keepdims=True) + RMS_EPS) * fg_ref[...]


def _post_call(o_fox, o_rwkv, gates, x, mod3, wfo, wro, wo, ln2_g, wup, conv_w, conv_b, wdn, final_g, tm):
    bsz, seq, _ = x.shape
    tok = lambda n: pl.BlockSpec((1, tm, n), lambda b, s: (b, s, 0))
    const = lambda shape: pl.BlockSpec(shape, lambda b, s: (0,) * len(shape),
                                       pipeline_mode=pl.Buffered(1))
    return pl.pallas_call(
        _post_kernel,
        grid=(bsz, seq // tm),
        in_specs=[tok(WIDTH), tok(WIDTH), tok(2 * D_MODEL), tok(D_MODEL),
                  pl.BlockSpec((1, 1, 6 * D_MODEL), lambda b, s: (b, 0, 0)),
                  const((WIDTH, D_MODEL)), const((WIDTH, D_MODEL)), const((D_MODEL, D_MODEL)),
                  const((1, D_MODEL)), const((D_MODEL, 2 * D_FF)), const((3, D_FF)), const((1, D_FF)),
                  const((D_FF, D_MODEL)), const((1, D_MODEL))],
        out_specs=tok(D_MODEL),
        out_shape=jax.ShapeDtypeStruct((bsz, seq, D_MODEL), jnp.float32),
        scratch_shapes=[pltpu.VMEM((8, D_FF), jnp.float32)],
        compiler_params=pltpu.CompilerParams(
            dimension_semantics=("arbitrary", "arbitrary"), vmem_limit_bytes=VMEM_LIMIT),
        name="post_ffn",
    )(o_fox, o_rwkv, gates, x, mod3, wfo, wro, wo, ln2_g, wup, conv_w, conv_b, wdn, final_g)


def _pad_cols(w, n):
    return jnp.pad(w, ((0, 0), (0, n - w.shape[1])))


def _layer(h, c, w_mod, b_mod, ln1_g, w_in, fox_b_f, fox_w_out, rwkv_mu, rwkv_w0, rwkv_w2, rwkv_a0,
           rwkv_a2, rwkv_g2, rwkv_k_k, rwkv_k_a, rwkv_r_k, rwkv_ln_w, rwkv_ln_b, rwkv_w_out, w_o,
           ln2_g, w_up, conv_w, conv_b, w_down, out_gain):
    bsz, seq, _ = h.shape
    bf16 = jnp.bfloat16
    lora = (LORA_PAD - 64)

    fox_cols = 3 * WIDTH + HEADS
    o_r = fox_cols
    o_g = o_r + 3 * WIDTH + 64 + 64 + GATE_LORA
    w_in_p = jnp.concatenate([
        w_in[:, 0:3 * WIDTH], _pad_cols(w_in[:, 3 * WIDTH:fox_cols], LANES),
        w_in[:, o_r:o_r + 3 * WIDTH],
        _pad_cols(w_in[:, o_r + 3 * WIDTH:o_r + 3 * WIDTH + 64], LORA_PAD),
        _pad_cols(w_in[:, o_r + 3 * WIDTH + 64:o_r + 3 * WIDTH + 128], LORA_PAD),
        w_in[:, o_r + 3 * WIDTH + 128:o_g], w_in[:, o_g:]], axis=1).astype(bf16)
    mu = rwkv_mu.reshape(1, -1)
    mu_p = jnp.concatenate([mu[:, 0:3 * WIDTH], _pad_cols(mu[:, 3 * WIDTH:3 * WIDTH + 64], LORA_PAD),
                            _pad_cols(mu[:, 3 * WIDTH + 64:3 * WIDTH + 128], LORA_PAD),
                            mu[:, 3 * WIDTH + 128:]], axis=1)
    w2_p = jnp.pad(rwkv_w2, ((0, lora), (0, 0))).astype(bf16)
    a2_p = jnp.pad(rwkv_a2, ((0, lora), (0, 0))).astype(bf16)
    bf_pad = _pad_cols(fox_b_f.reshape(1, HEADS), LANES)
    rowv = lambda t: t.reshape(1, -1)

    tm_in = min(512, seq)
    tq = min(512, seq)
    tm_rw = min(256, seq)
    tm_post = min(512, seq)

    mod = _mod_call(c, w_mod, b_mod)
    mod3 = mod.reshape(bsz, 1, 6 * D_MODEL)
    q_aug, k_aug, v_aug, p_rwkv, gates = _inproj_call(h, mod3, rowv(ln1_g), bf_pad, w_in_p, tm_in)
    o_fox = _fox_call(q_aug, k_aug, v_aug, tq)
    o_rwkv = _rwkv_call(p_rwkv, mu_p, rowv(rwkv_w0), w2_p, rowv(rwkv_a0), a2_p, rwkv_g2.astype(bf16),
                        rowv(rwkv_k_k), rowv(rwkv_k_a), rowv(rwkv_r_k), rowv(rwkv_ln_w),
                        rowv(rwkv_ln_b), tm_rw)
    return _post_call(o_fox, o_rwkv, gates, h, mod3, fox_w_out.astype(bf16), rwkv_w_out.astype(bf16),
                      w_o.astype(bf16), rowv(ln2_g), w_up.astype(bf16), conv_w, rowv(conv_b),
                      w_down.astype(bf16), rowv(out_gain), tm_post)


def kernel(x, c, w_mod, b_mod, ln1_g, w_in, fox_b_f, fox_w_out, rwkv_mu, rwkv_w0, rwkv_w2, rwkv_a0,
           rwkv_a2, rwkv_g2, rwkv_k_k, rwkv_k_a, rwkv_r_k, rwkv_ln_w, rwkv_ln_b, rwkv_w_out, w_o,
           ln2_g, w_up, conv_w, conv_b, w_down, final_g):
    depth = w_mod.shape[0]
    assert depth == 1, "single trunk layer: the final RMSNorm is fused into the layer's last kernel"
    return _layer(x, c, w_mod[0], b_mod[0], ln1_g[0], w_in[0], fox_b_f[0], fox_w_out[0], rwkv_mu[0],
                  rwkv_w0[0], rwkv_w2[0], rwkv_a0[0], rwkv_a2[0], rwkv_g2[0], rwkv_k_k[0], rwkv_k_a[0],
                  rwkv_r_k[0], rwkv_ln_w[0], rwkv_ln_b[0], rwkv_w_out[0], w_o[0], ln2_g[0], w_up[0],
                  conv_w[0], conv_b[0], w_down[0], final_g)
```

```python
import functools

import jax
import jax.numpy as jnp
from jax import lax
from jax.experimental import pallas as pl
from jax.experimental.pallas import tpu as pltpu

D_MODEL = 1024
HEADS = 8
HEAD_DIM = 64
WIDTH = HEADS * HEAD_DIM
LORA_PAD = 128
GATE_LORA = 128
D_FF = 2816
RMS_EPS = 1e-6
GN_EPS = HEAD_DIM * 1e-5
L2_EPS = 1e-12
LANES = 128
CHUNK = 64

OFF_QKV = 0
OFF_RWKV = 3 * WIDTH
RWKV_P = 3 * WIDTH + 2 * LORA_PAD + GATE_LORA
OFF_F = OFF_RWKV + RWKV_P
OFF_GATE = OFF_F + LANES
IN_P = OFF_GATE + 2 * D_MODEL

VMEM_LIMIT = 56 * 1024 * 1024

HI = lax.Precision.HIGHEST
NT = (((1,), (1,)), ((), ()))


def _bdot(a, b):
    return jnp.dot(a.astype(jnp.bfloat16), b.astype(jnp.bfloat16), preferred_element_type=jnp.float32)


def _sigmoid(x):
    return 1.0 / (1.0 + jnp.exp(-x))


def _softplus(x):
    return jnp.maximum(x, 0.0) + jnp.log(1.0 + jnp.exp(-jnp.abs(x)))


def _tri(n, strict):
    r = lax.broadcasted_iota(jnp.int32, (n, n), 0)
    c = lax.broadcasted_iota(jnp.int32, (n, n), 1)
    return (r > c) if strict else (r >= c)


def _head_sum(x):
    lane = lax.broadcasted_iota(jnp.int32, (x.shape[0], LANES), 1)
    lo = lane < HEAD_DIM
    outs = []
    for g in range(x.shape[1] // LANES):
        xg = x[:, g * LANES:(g + 1) * LANES]
        s0 = jnp.sum(jnp.where(lo, xg, 0.0), axis=-1, keepdims=True)
        s1 = jnp.sum(jnp.where(lo, 0.0, xg), axis=-1, keepdims=True)
        outs.append(jnp.where(lo, s0, s1))
    return jnp.concatenate(outs, axis=-1)


def _mod_kernel(c_ref, w_ref, b_ref, o_ref):
    c = c_ref[...]
    sc = c * _sigmoid(c)
    o_ref[...] = jnp.dot(sc, w_ref[...], precision=HI, preferred_element_type=jnp.float32) + b_ref[...]


def _mod_call(c, w_mod, b_mod):
    bsz = c.shape[0]
    n = w_mod.shape[1]
    tn = 1024
    return pl.pallas_call(
        _mod_kernel,
        grid=(n // tn,),
        in_specs=[pl.BlockSpec((bsz, D_MODEL), lambda j: (0, 0)),
                  pl.BlockSpec((D_MODEL, tn), lambda j: (0, j)),
                  pl.BlockSpec((1, tn), lambda j: (0, j))],
        out_specs=pl.BlockSpec((bsz, tn), lambda j: (0, j)),
        out_shape=jax.ShapeDtypeStruct((bsz, n), jnp.float32),
        name="mod",
    )(c, w_mod, b_mod.reshape(1, n))


def _split3(c):
    f32 = jnp.float32
    hi = c.astype(jnp.bfloat16).astype(f32)
    mid = (c - hi).astype(jnp.bfloat16).astype(f32)
    lo = (c - hi - mid).astype(jnp.bfloat16).astype(f32)
    return hi, mid, lo


def _cumsum_rows(tri, x):
    return sum(jnp.dot(tri, part.astype(jnp.bfloat16), preferred_element_type=jnp.float32)
               for part in _split3(x))


def _inproj_kernel(x_ref, mod_ref, g_ref, bf_ref, w_ref,
                   q_ref, k_ref, v_ref, p_ref, gate_ref, carry_ref):
    s = pl.program_id(1)
    tm = x_ref.shape[1]

    @pl.when(s == 0)
    def _():
        carry_ref[...] = jnp.zeros_like(carry_ref)

    x = x_ref[0]
    mod = mod_ref[0]
    shift = mod[:, 0:D_MODEL]
    scale = mod[:, D_MODEL:2 * D_MODEL]
    y = x * lax.rsqrt(jnp.mean(x * x, axis=-1, keepdims=True) + RMS_EPS) * g_ref[...]
    u = (y * (1.0 + scale) + shift).astype(jnp.bfloat16)

    def proj(off, n):
        return jnp.dot(u, w_ref[:, off:off + n], preferred_element_type=jnp.float32)

    for off in range(0, 3 * WIDTH, WIDTH):
        p_ref[0, :, off:off + WIDTH] = proj(OFF_RWKV + off, WIDTH)
    tail = proj(OFF_RWKV + 3 * WIDTH, WIDTH)
    p_ref[0, :, 3 * WIDTH:RWKV_P] = tail[:, 0:RWKV_P - 3 * WIDTH]
    z = tail[:, RWKV_P - 3 * WIDTH:] + bf_ref[...]
    logf = jnp.minimum(z, 0.0) - jnp.log(1.0 + jnp.exp(-jnp.abs(z)))
    cum = _cumsum_rows(_tri(tm, strict=False).astype(jnp.bfloat16), logf) + carry_ref[...]
    carry_ref[...] = cum[tm - 1:tm, :]

    q = proj(OFF_QKV, WIDTH) * (HEAD_DIM ** -0.5)
    k = proj(OFF_QKV + WIDTH, WIDTH)
    v = proj(OFF_QKV + 2 * WIDTH, WIDTH)
    lane = lax.broadcasted_iota(jnp.int32, (tm, LANES), 1)
    lo = lane < HEAD_DIM
    for h in range(HEADS):
        g = slice((h // 2) * LANES, (h // 2 + 1) * LANES)
        take = (lambda t: t[:, g]) if h % 2 == 0 else (lambda t: pltpu.roll(t[:, g], HEAD_DIM, 1))
        c_hi, c_mid, c_lo = _split3(cum[:, h:h + 1])
        parts = jnp.where(lane == HEAD_DIM, c_hi, jnp.where(lane == HEAD_DIM + 1, c_mid, c_lo))
        q_bias = jnp.where(lane < HEAD_DIM + 3, parts, jnp.where(lane < HEAD_DIM + 6, -1.0, 0.0))
        parts = jnp.where(lane == HEAD_DIM + 3, c_hi, jnp.where(lane == HEAD_DIM + 4, c_mid, c_lo))
        k_bias = jnp.where(lane < HEAD_DIM + 3, 1.0, jnp.where(lane < HEAD_DIM + 6, parts, 0.0))
        v_one = jnp.where(lane == HEAD_DIM, 1.0, 0.0)
        q_ref[0, h] = jnp.where(lo, take(q), q_bias).astype(jnp.bfloat16)
        k_ref[0, h] = jnp.where(lo, take(k), k_bias).astype(jnp.bfloat16)
        v_ref[0, h] = jnp.where(lo, take(v), v_one).astype(jnp.bfloat16)

    for off in range(0, 2 * D_MODEL, 512):
        gate_ref[0, :, off:off + 512] = _sigmoid(proj(OFF_GATE + off, 512)).astype(jnp.bfloat16)


def _inproj_call(x, mod3, ln1_g, bf_pad, w_in_p, tm):
    bsz, seq, _ = x.shape
    grid = (bsz, seq // tm)
    tok = lambda n: pl.BlockSpec((1, tm, n), lambda b, s: (b, s, 0))
    const = lambda shape: pl.BlockSpec(shape, lambda b, s: (0,) * len(shape))
    head = pl.BlockSpec((1, HEADS, tm, LANES), lambda b, s: (b, 0, s, 0))
    out_shapes = (
        jax.ShapeDtypeStruct((bsz, HEADS, seq, LANES), jnp.bfloat16),
        jax.ShapeDtypeStruct((bsz, HEADS, seq, LANES), jnp.bfloat16),
        jax.ShapeDtypeStruct((bsz, HEADS, seq, LANES), jnp.bfloat16),
        jax.ShapeDtypeStruct((bsz, seq, RWKV_P), jnp.float32),
        jax.ShapeDtypeStruct((bsz, seq, 2 * D_MODEL), jnp.bfloat16),
    )
    return pl.pallas_call(
        _inproj_kernel,
        grid=grid,
        in_specs=[tok(D_MODEL),
                  pl.BlockSpec((1, 1, 6 * D_MODEL), lambda b, s: (b, 0, 0)),
                  const((1, D_MODEL)), const((1, LANES)), const((D_MODEL, IN_P))],
        out_specs=(head, head, head, tok(RWKV_P), tok(2 * D_MODEL)),
        out_shape=out_shapes,
        scratch_shapes=[pltpu.VMEM((1, LANES), jnp.float32)],
        compiler_params=pltpu.CompilerParams(
            dimension_semantics=("arbitrary", "arbitrary"), vmem_limit_bytes=VMEM_LIMIT),
        name="in_proj",
    )(x, mod3, ln1_g, bf_pad, w_in_p)


NEG = -1e30


def _fox_kernel(q_ref, k_ref, v_ref, o_ref, sa_ref, sb_ref, *, tq):
    i = pl.program_id(2)
    causal = _tri(tq, strict=False)
    qs = [q_ref[0, hh] for hh in range(2)]

    def logits(j, s_ref):
        j0 = pl.multiple_of(j * tq, tq)
        for hh in range(2):
            s_ref[hh] = lax.dot_general(qs[hh], k_ref[0, hh, pl.ds(j0, tq), :], NT,
                                        preferred_element_type=jnp.float32)

    def softmax_pv(j, s_ref, carry, masked):
        j0 = pl.multiple_of(j * tq, tq)
        new = []
        for hh in range(2):
            m, acc = carry[hh]
            sc = s_ref[hh]
            if masked:
                sc = jnp.where(causal, sc, NEG)
            m_new = jnp.maximum(m, jnp.max(sc, axis=-1, keepdims=True))
            p = jnp.exp(sc - m_new).astype(jnp.bfloat16)
            acc = jnp.exp(m - m_new) * acc + jnp.dot(p, v_ref[0, hh, pl.ds(j0, tq), :],
                                                     preferred_element_type=jnp.float32)
            new.append((m_new, acc))
        return tuple(new)

    init = tuple((jnp.full((tq, 1), NEG, jnp.float32), jnp.zeros((tq, LANES), jnp.float32))
                 for _ in range(2))
    logits(0, sa_ref)

    def two_blocks(t, carry):
        logits(2 * t + 1, sb_ref)
        carry = softmax_pv(2 * t, sa_ref, carry, False)
        logits(2 * t + 2, sa_ref)
        return softmax_pv(2 * t + 1, sb_ref, carry, False)

    carry = lax.fori_loop(0, i // 2, two_blocks, init)

    def last_odd(carry):
        logits(i, sb_ref)
        carry = softmax_pv(i - 1, sa_ref, carry, False)
        return softmax_pv(i, sb_ref, carry, True)

    def last_even(carry):
        return softmax_pv(i, sa_ref, carry, True)

    (_, acc0), (_, acc1) = lax.cond(i % 2 == 1, last_odd, last_even, carry)
    o0 = acc0 / acc0[:, HEAD_DIM:HEAD_DIM + 1]
    o1 = pltpu.roll(acc1 / acc1[:, HEAD_DIM:HEAD_DIM + 1], HEAD_DIM, 1)
    lane = lax.broadcasted_iota(jnp.int32, (tq, LANES), 1)
    o_ref[0] = jnp.where(lane < HEAD_DIM, o0, o1).astype(o_ref.dtype)


def _fox_call(q_aug, k_aug, v_aug, tq):
    bsz, _, seq, _ = q_aug.shape
    npair = HEADS // 2
    kv_spec = pl.BlockSpec((1, 2, seq, LANES), lambda b, p, i: (b, p, 0, 0))
    return pl.pallas_call(
        functools.partial(_fox_kernel, tq=tq),
        grid=(bsz, npair, seq // tq),
        in_specs=[pl.BlockSpec((1, 2, tq, LANES), lambda b, p, i: (b, p, i, 0)), kv_spec, kv_spec],
        out_specs=pl.BlockSpec((1, tq, LANES), lambda b, p, i: (b, i, p)),
        out_shape=jax.ShapeDtypeStruct((bsz, seq, WIDTH), jnp.bfloat16),
        scratch_shapes=[pltpu.VMEM((2, tq, tq), jnp.float32), pltpu.VMEM((2, tq, tq), jnp.float32)],
        compiler_params=pltpu.CompilerParams(
            dimension_semantics=("arbitrary", "arbitrary", "arbitrary"), vmem_limit_bytes=VMEM_LIMIT),
        name="fox_attn",
    )(q_aug, k_aug, v_aug)


BMM = (((2,), (1,)), ((0,), (0,)))


def _bmm(a, b):
    return lax.dot_general(a.astype(jnp.bfloat16), b.astype(jnp.bfloat16), BMM,
                           preferred_element_type=jnp.float32)


def _rwkv_kernel(p_ref, mu_ref, w0_ref, w2_ref, a0_ref, a2_ref, g2_ref, kk_ref, ka_ref, rk_ref,
                 lnw_ref, lnb_ref, o_ref,
                 st_ref, carry_ref, lhs_ref, rhst_ref, v_ref, wc_ref, y_ref):
    s = pl.program_id(1)
    tm = p_ref.shape[1]
    nchunk = tm // CHUNK

    @pl.when(s == 0)
    def _():
        carry_ref[...] = jnp.zeros_like(carry_ref)
        st_ref[...] = jnp.zeros_like(st_ref)

    p = p_ref[0]
    row = lax.broadcasted_iota(jnp.int32, p.shape, 0)
    prev = jnp.where(row == 0, carry_ref[...], pltpu.roll(p, 1, 0))
    carry_ref[...] = p[tm - 1:tm, :]
    xl = p + (prev - p) * mu_ref[...]
    r = xl[:, 0:WIDTH]
    k = xl[:, WIDTH:2 * WIDTH]
    v = xl[:, 2 * WIDTH:3 * WIDTH]
    xw = xl[:, 3 * WIDTH:3 * WIDTH + LORA_PAD]
    xa = xl[:, 3 * WIDTH + LORA_PAD:3 * WIDTH + 2 * LORA_PAD]
    xg = xl[:, 3 * WIDTH + 2 * LORA_PAD:RWKV_P]

    zw = w0_ref[...] + _bdot(jnp.tanh(xw), w2_ref[...])
    lw = -jnp.exp(-_softplus(-zw) - 0.5)
    al = _sigmoid(a0_ref[...] + _bdot(xa, a2_ref[...]))
    g = _bdot(_sigmoid(xg), g2_ref[...])

    kk = k * kk_ref[...]
    kk = kk / jnp.maximum(jnp.sqrt(_head_sum(kk * kk)), L2_EPS)
    k_mod = k * (1.0 + (al - 1.0) * ka_ref[...])
    b_vec = kk * al
    bonus = _head_sum(r * k_mod * rk_ref[...]) * v

    tri = _tri(CHUNK, strict=False).astype(jnp.bfloat16)
    for c in range(nchunk):
        sl = slice(c * CHUNK, (c + 1) * CHUNK)
        lwc = lw[sl]
        cl = _cumsum_rows(tri, lwc)
        el = jnp.exp(cl)
        inv = jnp.exp(-cl)
        lhs = jnp.concatenate([-kk[sl] * jnp.exp(cl - lwc), r[sl] * el], axis=0)
        rhst = jnp.concatenate([b_vec[sl] * inv, k_mod[sl] * inv], axis=0).T
        wct = el[CHUNK - 8:CHUNK, :].T[:, 7:8]
        vc = v[sl]
        for h in range(HEADS):
            i = c * HEADS + h
            hs = slice(h * HEAD_DIM, (h + 1) * HEAD_DIM)
            lhs_ref[i] = lhs[:, hs].astype(jnp.bfloat16)
            rhst_ref[i] = rhst[hs, :].astype(jnp.bfloat16)
            v_ref[i] = vc[:, hs].astype(jnp.bfloat16)
            wc_ref[i] = jnp.broadcast_to(wct[hs, :], (HEAD_DIM, HEAD_DIM))

    r2 = lax.broadcasted_iota(jnp.int32, (2 * CHUNK, 2 * CHUNK), 0)
    c2 = lax.broadcasted_iota(jnp.int32, (2 * CHUNK, 2 * CHUNK), 1)
    tt = r2 & (CHUNK - 1)
    jj = c2 & (CHUNK - 1)
    aa_mask = (tt > jj) | ((r2 >= CHUNK) & (tt == jj))
    eye = (_tri(CHUNK, False) & ~_tri(CHUNK, True)).astype(jnp.float32)
    ri = lax.broadcasted_iota(jnp.int32, (CHUNK, CHUNK), 0)
    ci = lax.broadcasted_iota(jnp.int32, (CHUNK, CHUNK), 1)
    lvl_masks = [((ri >> (l + 1)) == (ci >> (l + 1))) & ((ri >> l) != (ci >> l))
                 for l in range(CHUNK.bit_length() - 1)]

    lhs = lhs_ref[...]
    aa = jnp.where(aa_mask, _bmm(lhs, rhst_ref[...]), 0.0)
    a_ab = aa[:, 0:CHUNK, 0:CHUNK]
    a_ak = aa[:, 0:CHUNK, CHUNK:]
    a_r = aa[:, CHUNK:, :]
    t = eye + jnp.where(lvl_masks[0], a_ab, 0.0)
    for lm in lvl_masks[1:]:
        t = t + _bmm(t, _bmm(jnp.where(lm, a_ab, 0.0), t))
    vb = v_ref[...]
    x0 = _bmm(a_ak, vb)
    tt = _bmm(t, jnp.concatenate([lhs[:, 0:CHUNK, :], x0.astype(jnp.bfloat16)], axis=-1))
    ta = tt[:, :, 0:HEAD_DIM]
    tx = tt[:, :, HEAD_DIM:]
    yl = jnp.concatenate([lhs[:, CHUNK:, :], a_r.astype(jnp.bfloat16)], axis=-1)

    st = st_ref[...]
    for c in range(nchunk):
        blk = slice(c * HEADS, (c + 1) * HEADS)
        sl = slice(c * CHUNK, (c + 1) * CHUNK)
        stb = st.astype(jnp.bfloat16)
        u = (_bmm(ta[blk], stb) + tx[blk]).astype(jnp.bfloat16)
        uv = jnp.concatenate([u, vb[blk]], axis=1)
        st = wc_ref[blk] * (st + _bmm(rhst_ref[blk], uv))
        y = _bmm(yl[blk], jnp.concatenate([stb, uv], axis=1))
        for h in range(HEADS):
            y_ref[c, :, h * HEAD_DIM:(h + 1) * HEAD_DIM] = y[h]
        yv = y_ref[c]
        mean = _head_sum(yv) * (1.0 / HEAD_DIM)
        d = yv - mean
        var = _head_sum(d * d) * (1.0 / HEAD_DIM)
        yn = d * lax.rsqrt(var + GN_EPS) * lnw_ref[...] + lnb_ref[...]
        o_ref[0, sl, :] = ((yn + bonus[sl]) * g[sl]).astype(o_ref.dtype)
    st_ref[...] = st


def _rwkv_call(p, mu_p, w0, w2_p, a0, a2_p, g2, k_k, k_a, r_k, ln_w, ln_b, tm):
    bsz, seq, _ = p.shape
    nchunk = tm // CHUNK
    nb = nchunk * HEADS
    row = lambda n: pl.BlockSpec((1, n), lambda b, s: (0, 0))
    mat = lambda m, n: pl.BlockSpec((m, n), lambda b, s: (0, 0))
    return pl.pallas_call(
        _rwkv_kernel,
        grid=(bsz, seq // tm),
        in_specs=[pl.BlockSpec((1, tm, RWKV_P), lambda b, s: (b, s, 0)),
                  row(RWKV_P), row(WIDTH), mat(LORA_PAD, WIDTH), row(WIDTH), mat(LORA_PAD, WIDTH),
                  mat(GATE_LORA, WIDTH), row(WIDTH), row(WIDTH), row(WIDTH), row(WIDTH), row(WIDTH)],
        out_specs=pl.BlockSpec((1, tm, WIDTH), lambda b, s: (b, s, 0)),
        out_shape=jax.ShapeDtypeStruct((bsz, seq, WIDTH), jnp.bfloat16),
        scratch_shapes=[pltpu.VMEM((HEADS, HEAD_DIM, HEAD_DIM), jnp.float32),
                        pltpu.VMEM((1, RWKV_P), jnp.float32),
                        pltpu.VMEM((nb, 2 * CHUNK, HEAD_DIM), jnp.bfloat16),
                        pltpu.VMEM((nb, HEAD_DIM, 2 * CHUNK), jnp.bfloat16),
                        pltpu.VMEM((nb, CHUNK, HEAD_DIM), jnp.bfloat16),
                        pltpu.VMEM((nb, HEAD_DIM, HEAD_DIM), jnp.float32),
                        pltpu.VMEM((nchunk, CHUNK, WIDTH), jnp.float32)],
        compiler_params=pltpu.CompilerParams(
            dimension_semantics=("arbitrary", "arbitrary"), vmem_limit_bytes=VMEM_LIMIT),
        name="rwkv7",
    )(p, mu_p, w0, w2_p, a0, a2_p, g2, k_k, k_a, r_k, ln_w, ln_b)


FF_CHUNK = 256


def _post_kernel(of_ref, or_ref, gate_ref, x_ref, mod_ref, wfo_ref, wro_ref, wo_ref, g2_ref,
                 wup_ref, cw_ref, cb_ref, wdn_ref, fg_ref, o_ref, carry_ref):
    s = pl.program_id(1)
    tm = x_ref.shape[1]

    @pl.when(s == 0)
    def _():
        carry_ref[...] = jnp.zeros_like(carry_ref)

    mod = mod_ref[0]
    gate1 = mod[:, 2 * D_MODEL:3 * D_MODEL]
    shift2 = mod[:, 3 * D_MODEL:4 * D_MODEL]
    scale2 = mod[:, 4 * D_MODEL:5 * D_MODEL]
    gate2 = mod[:, 5 * D_MODEL:6 * D_MODEL]

    gate = gate_ref[0]
    m_fox = jnp.dot(of_ref[0], wfo_ref[...], preferred_element_type=jnp.float32)
    m_rwkv = jnp.dot(or_ref[0], wro_ref[...], preferred_element_type=jnp.float32)
    merged = gate[:, 0:D_MODEL] * m_fox + gate[:, D_MODEL:] * m_rwkv
    mix = jnp.dot(merged.astype(jnp.bfloat16), wo_ref[...], preferred_element_type=jnp.float32)
    h1 = x_ref[0] + gate1 * mix

    y = h1 * lax.rsqrt(jnp.mean(h1 * h1, axis=-1, keepdims=True) + RMS_EPS) * g2_ref[...]
    u2 = (y * (1.0 + scale2) + shift2).astype(jnp.bfloat16)

    row = lax.broadcasted_iota(jnp.int32, (tm, FF_CHUNK), 0)
    def up(c):
        lo = c * FF_CHUNK
        return (jnp.dot(u2, wup_ref[:, lo:lo + FF_CHUNK], preferred_element_type=jnp.float32),
                jnp.dot(u2, wup_ref[:, D_FF + lo:D_FF + lo + FF_CHUNK], preferred_element_type=jnp.float32))

    acc = jnp.zeros((tm, D_MODEL), jnp.float32)
    nxt = up(0)
    for c in range(D_FF // FF_CHUNK):
        cs = slice(c * FF_CHUNK, (c + 1) * FF_CHUNK)
        a, b = nxt
        if c + 1 < D_FF // FF_CHUNK:
            nxt = up(c + 1)
        tail = carry_ref[:, cs]
        p1 = tail[7:8, :]
        p2 = tail[6:7, :]
        a1 = jnp.where(row == 0, p1, pltpu.roll(a, 1, 0))
        a2 = jnp.where(row == 0, p2, jnp.where(row == 1, p1, pltpu.roll(a, 2, 0)))
        carry_ref[:, cs] = a[tm - 8:tm, :]
        cw = cw_ref[:, cs]
        conv = cw[0:1, :] * a2 + cw[1:2, :] * a1 + cw[2:3, :] * a + cb_ref[:, cs]
        act = conv * _sigmoid(conv) * b
        acc = acc + jnp.dot(act.astype(jnp.bfloat16), wdn_ref[cs, :], preferred_element_type=jnp.float32)

    h2 = h1 + gate2 * acc
    o_ref[0] = h2 * lax.rsqrt(jnp.mean(h2 * h2, axis=-1, keepdims=True) + RMS_EPS) * fg_ref[...]


def _post_call(o_fox, o_rwkv, gates, x, mod3, wfo, wro, wo, ln2_g, wup, conv_w, conv_b, wdn, final_g, tm):
    bsz, seq, _ = x.shape
    tok = lambda n: pl.BlockSpec((1, tm, n), lambda b, s: (b, s, 0))
    const = lambda shape: pl.BlockSpec(shape, lambda b, s: (0,) * len(shape),
                                       pipeline_mode=pl.Buffered(1))
    return pl.pallas_call(
        _post_kernel,
        grid=(bsz, seq // tm),
        in_specs=[tok(WIDTH), tok(WIDTH), tok(2 * D_MODEL), tok(D_MODEL),
                  pl.BlockSpec((1, 1, 6 * D_MODEL), lambda b, s: (b, 0, 0)),
                  const((WIDTH, D_MODEL)), const((WIDTH, D_MODEL)), const((D_MODEL, D_MODEL)),
                  const((1, D_MODEL)), const((D_MODEL, 2 * D_FF)), const((3, D_FF)), const((1, D_FF)),
                  const((D_FF, D_MODEL)), const((1, D_MODEL))],
        out_specs=tok(D_MODEL),
        out_shape=jax.ShapeDtypeStruct((bsz, seq, D_MODEL), jnp.float32),
        scratch_shapes=[pltpu.VMEM((8, D_FF), jnp.float32)],
        compiler_params=pltpu.CompilerParams(
            dimension_semantics=("arbitrary", "arbitrary"), vmem_limit_bytes=VMEM_LIMIT),
        name="post_ffn",
    )(o_fox, o_rwkv, gates, x, mod3, wfo, wro, wo, ln2_g, wup, conv_w, conv_b, wdn, final_g)


def _pad_cols(w, n):
    return jnp.pad(w, ((0, 0), (0, n - w.shape[1])))


def _layer(h, c, w_mod, b_mod, ln1_g, w_in, fox_b_f, fox_w_out, rwkv_mu, rwkv_w0, rwkv_w2, rwkv_a0,
           rwkv_a2, rwkv_g2, rwkv_k_k, rwkv_k_a, rwkv_r_k, rwkv_ln_w, rwkv_ln_b, rwkv_w_out, w_o,
           ln2_g, w_up, conv_w, conv_b, w_down, out_gain):
    bsz, seq, _ = h.shape
    bf16 = jnp.bfloat16
    lora = (LORA_PAD - 64)

    fox_cols = 3 * WIDTH + HEADS
    o_r = fox_cols
    o_g = o_r + 3 * WIDTH + 64 + 64 + GATE_LORA
    w_in_p = jnp.concatenate([
        w_in[:, 0:3 * WIDTH],
        w_in[:, o_r:o_r + 3 * WIDTH],
        _pad_cols(w_in[:, o_r + 3 * WIDTH:o_r + 3 * WIDTH + 64], LORA_PAD),
        _pad_cols(w_in[:, o_r + 3 * WIDTH + 64:o_r + 3 * WIDTH + 128], LORA_PAD),
        w_in[:, o_r + 3 * WIDTH + 128:o_g],
        _pad_cols(w_in[:, 3 * WIDTH:fox_cols], LANES),
        w_in[:, o_g:]], axis=1).astype(bf16)
    mu = rwkv_mu.reshape(1, -1)
    mu_p = jnp.concatenate([mu[:, 0:3 * WIDTH], _pad_cols(mu[:, 3 * WIDTH:3 * WIDTH + 64], LORA_PAD),
                            _pad_cols(mu[:, 3 * WIDTH + 64:3 * WIDTH + 128], LORA_PAD),
                            mu[:, 3 * WIDTH + 128:]], axis=1)
    w2_p = jnp.pad(rwkv_w2, ((0, lora), (0, 0))).astype(bf16)
    a2_p = jnp.pad(rwkv_a2, ((0, lora), (0, 0))).astype(bf16)
    bf_pad = _pad_cols(fox_b_f.reshape(1, HEADS), LANES)
    rowv = lambda t: t.reshape(1, -1)

    tm_in = min(512, seq)
    tq = min(512, seq)
    tm_rw = min(256, seq)
    tm_post = min(512, seq)

    mod = _mod_call(c, w_mod, b_mod)
    mod3 = mod.reshape(bsz, 1, 6 * D_MODEL)
    q_aug, k_aug, v_aug, p_rwkv, gates = _inproj_call(h, mod3, rowv(ln1_g), bf_pad, w_in_p, tm_in)
    o_fox = _fox_call(q_aug, k_aug, v_aug, tq)
    o_rwkv = _rwkv_call(p_rwkv, mu_p, rowv(rwkv_w0), w2_p, rowv(rwkv_a0), a2_p, rwkv_g2.astype(bf16),
                        rowv(rwkv_k_k), rowv(rwkv_k_a), rowv(rwkv_r_k), rowv(rwkv_ln_w),
                        rowv(rwkv_ln_b), tm_rw)
    return _post_call(o_fox, o_rwkv, gates, h, mod3, fox_w_out.astype(bf16), rwkv_w_out.astype(bf16),
                      w_o.astype(bf16), rowv(ln2_g), w_up.astype(bf16), conv_w, rowv(conv_b),
                      w_down.astype(bf16), rowv(out_gain), tm_post)


def kernel(x, c, w_mod, b_mod, ln1_g, w_in, fox_b_f, fox_w_out, rwkv_mu, rwkv_w0, rwkv_w2, rwkv_a0,
           rwkv_a2, rwkv_g2, rwkv_k_k, rwkv_k_a, rwkv_r_k, rwkv_ln_w, rwkv_ln_b, rwkv_w_out, w_o,
           ln2_g, w_up, conv_w, conv_b, w_down, final_g):
    depth = w_mod.shape[0]
    assert depth == 1, "single trunk layer: the final RMSNorm is fused into the layer's last kernel"
    return _layer(x, c, w_mod[0], b_mod[0], ln1_g[0], w_in[0], fox_b_f[0], fox_w_out[0], rwkv_mu[0],
                  rwkv_w0[0], rwkv_w2[0], rwkv_a0[0], rwkv_a2[0], rwkv_g2[0], rwkv_k_k[0], rwkv_k_a[0],
                  rwkv_r_k[0], rwkv_ln_w[0], rwkv_ln_b[0], rwkv_w_out[0], w_o[0], ln2_g[0], w_up[0],
                  conv_w[0], conv_b[0], w_down[0], final_g)
```

```python
import functools

import jax
import jax.numpy as jnp
from jax import lax
from jax.experimental import pallas as pl
from jax.experimental.pallas import tpu as pltpu

D_MODEL = 1024
HEADS = 8
HEAD_DIM = 64
WIDTH = HEADS * HEAD_DIM
LORA_PAD = 128
GATE_LORA = 128
D_FF = 2816
RMS_EPS = 1e-6
GN_EPS = HEAD_DIM * 1e-5
L2_EPS = 1e-12
LANES = 128
CHUNK = 64

OFF_QKV = 0
OFF_RWKV = 3 * WIDTH
RWKV_P = 3 * WIDTH + 2 * LORA_PAD + GATE_LORA
OFF_F = OFF_RWKV + RWKV_P
OFF_GATE = OFF_F + LANES
IN_P = OFF_GATE + 2 * D_MODEL

VMEM_LIMIT = 56 * 1024 * 1024

HI = lax.Precision.HIGHEST
NT = (((1,), (1,)), ((), ()))


def _bdot(a, b):
    return jnp.dot(a.astype(jnp.bfloat16), b.astype(jnp.bfloat16), preferred_element_type=jnp.float32)


def _sigmoid(x):
    return 1.0 / (1.0 + jnp.exp(-x))


def _softplus(x):
    return jnp.maximum(x, 0.0) + jnp.log(1.0 + jnp.exp(-jnp.abs(x)))


def _tri(n, strict):
    r = lax.broadcasted_iota(jnp.int32, (n, n), 0)
    c = lax.broadcasted_iota(jnp.int32, (n, n), 1)
    return (r > c) if strict else (r >= c)


def _head_sum(x):
    lane = lax.broadcasted_iota(jnp.int32, (x.shape[0], LANES), 1)
    lo = lane < HEAD_DIM
    outs = []
    for g in range(x.shape[1] // LANES):
        xg = x[:, g * LANES:(g + 1) * LANES]
        s0 = jnp.sum(jnp.where(lo, xg, 0.0), axis=-1, keepdims=True)
        s1 = jnp.sum(jnp.where(lo, 0.0, xg), axis=-1, keepdims=True)
        outs.append(jnp.where(lo, s0, s1))
    return jnp.concatenate(outs, axis=-1)


def _mod_kernel(c_ref, w_ref, b_ref, o_ref):
    c = c_ref[...]
    sc = c * _sigmoid(c)
    o_ref[...] = jnp.dot(sc, w_ref[...], precision=HI, preferred_element_type=jnp.float32) + b_ref[...]


def _mod_call(c, w_mod, b_mod):
    bsz = c.shape[0]
    n = w_mod.shape[1]
    tn = 1024
    return pl.pallas_call(
        _mod_kernel,
        grid=(n // tn,),
        in_specs=[pl.BlockSpec((bsz, D_MODEL), lambda j: (0, 0)),
                  pl.BlockSpec((D_MODEL, tn), lambda j: (0, j)),
                  pl.BlockSpec((1, tn), lambda j: (0, j))],
        out_specs=pl.BlockSpec((bsz, tn), lambda j: (0, j)),
        out_shape=jax.ShapeDtypeStruct((bsz, n), jnp.float32),
        name="mod",
    )(c, w_mod, b_mod.reshape(1, n))


def _split3(c):
    f32 = jnp.float32
    hi = c.astype(jnp.bfloat16).astype(f32)
    mid = (c - hi).astype(jnp.bfloat16).astype(f32)
    lo = (c - hi - mid).astype(jnp.bfloat16).astype(f32)
    return hi, mid, lo


def _cumsum_rows(tri, x):
    return sum(jnp.dot(tri, part.astype(jnp.bfloat16), preferred_element_type=jnp.float32)
               for part in _split3(x))


def _inproj_kernel(x_ref, mod_ref, g_ref, bf_ref, w_ref,
                   mu_ref, w0_ref, w2_ref, a0_ref, a2_ref, g2_ref, kk_ref, ka_ref, rk_ref,
                   q_ref, k_ref, v_ref, gate_ref, lhs_ref, rhst_ref, vv_ref, gb_ref, wc_ref,
                   carry_ref, pcarry_ref):
    s = pl.program_id(1)
    tm = x_ref.shape[1]

    @pl.when(s == 0)
    def _():
        carry_ref[...] = jnp.zeros_like(carry_ref)
        pcarry_ref[...] = jnp.zeros_like(pcarry_ref)

    x = x_ref[0]
    mod = mod_ref[0]
    shift = mod[:, 0:D_MODEL]
    scale = mod[:, D_MODEL:2 * D_MODEL]
    y = x * lax.rsqrt(jnp.mean(x * x, axis=-1, keepdims=True) + RMS_EPS) * g_ref[...]
    u = (y * (1.0 + scale) + shift).astype(jnp.bfloat16)

    def proj(off, n):
        return jnp.dot(u, w_ref[:, off:off + n], preferred_element_type=jnp.float32)

    row = lax.broadcasted_iota(jnp.int32, (tm, WIDTH), 0)

    def shifted(off):
        cur = proj(OFF_RWKV + off, WIDTH)
        prev = jnp.where(row == 0, pcarry_ref[:, off:off + WIDTH], pltpu.roll(cur, 1, 0))
        pcarry_ref[:, off:off + WIDTH] = cur[tm - 1:tm, :]
        return cur, cur + (prev - cur) * mu_ref[:, off:off + WIDTH]

    def gate(i):
        gate_ref[0, :, i * 512:(i + 1) * 512] = _sigmoid(proj(OFF_GATE + i * 512, 512)).astype(jnp.bfloat16)

    _, r = shifted(0)
    _, k_r = shifted(WIDTH)
    _, v_r = shifted(2 * WIDTH)
    tail, tail_l = shifted(3 * WIDTH)
    xw = tail_l[:, 0:LORA_PAD]
    xa = tail_l[:, LORA_PAD:2 * LORA_PAD]
    xg = tail_l[:, 2 * LORA_PAD:2 * LORA_PAD + GATE_LORA]
    gate(0)
    zw = w0_ref[...] + _bdot(jnp.tanh(xw), w2_ref[...])
    lw = -jnp.exp(-_softplus(-zw) - 0.5)
    al = _sigmoid(a0_ref[...] + _bdot(xa, a2_ref[...]))
    gb_ref[0, :, 0:WIDTH] = _bdot(_sigmoid(xg), g2_ref[...])
    gate(1)
    kk = k_r * kk_ref[...]
    kk = kk / jnp.maximum(jnp.sqrt(_head_sum(kk * kk)), L2_EPS)
    gate(2)
    k_mod = k_r * (1.0 + (al - 1.0) * ka_ref[...])
    b_vec = kk * al
    gb_ref[0, :, WIDTH:] = _head_sum(r * k_mod * rk_ref[...]) * v_r
    vv_ref[0] = v_r.astype(jnp.bfloat16)
    gate(3)

    z = tail[:, RWKV_P - 3 * WIDTH:] + bf_ref[...]
    logf = jnp.minimum(z, 0.0) - jnp.log(1.0 + jnp.exp(-jnp.abs(z)))
    cum = _cumsum_rows(_tri(tm, strict=False).astype(jnp.bfloat16), logf) + carry_ref[...]
    carry_ref[...] = cum[tm - 1:tm, :]

    tri = _tri(CHUNK, strict=False).astype(jnp.bfloat16)

    def chunk(c):
        sl = slice(c * CHUNK, (c + 1) * CHUNK)
        lwc = lw[sl]
        cl = _cumsum_rows(tri, lwc)
        el = jnp.exp(cl)
        inv = jnp.exp(-cl)
        lhs_ref[0, sl, 0:WIDTH] = (-kk[sl] * jnp.exp(cl - lwc)).astype(jnp.bfloat16)
        lhs_ref[0, sl, WIDTH:] = (r[sl] * el).astype(jnp.bfloat16)
        rhs = jnp.concatenate([b_vec[sl] * inv, k_mod[sl] * inv], axis=0)
        rhst_ref[0, c] = rhs.T.astype(jnp.bfloat16)
        wc_ref[0, c] = el[CHUNK - 1:CHUNK, :]

    nck = tm // CHUNK
    cuts = [0, (nck + 2) // 3, (2 * nck + 2) // 3, nck]
    q = proj(OFF_QKV, WIDTH) * (HEAD_DIM ** -0.5)
    for c in range(cuts[0], cuts[1]):
        chunk(c)
    k = proj(OFF_QKV + WIDTH, WIDTH)
    for c in range(cuts[1], cuts[2]):
        chunk(c)
    v = proj(OFF_QKV + 2 * WIDTH, WIDTH)
    for c in range(cuts[2], cuts[3]):
        chunk(c)
    lane = lax.broadcasted_iota(jnp.int32, (tm, LANES), 1)
    lo = lane < HEAD_DIM
    for h in range(HEADS):
        g = slice((h // 2) * LANES, (h // 2 + 1) * LANES)
        take = (lambda t: t[:, g]) if h % 2 == 0 else (lambda t: pltpu.roll(t[:, g], HEAD_DIM, 1))
        c_hi, c_mid, c_lo = _split3(cum[:, h:h + 1])
        parts = jnp.where(lane == HEAD_DIM, c_hi, jnp.where(lane == HEAD_DIM + 1, c_mid, c_lo))
        q_bias = jnp.where(lane < HEAD_DIM + 3, parts, jnp.where(lane < HEAD_DIM + 6, -1.0, 0.0))
        parts = jnp.where(lane == HEAD_DIM + 3, c_hi, jnp.where(lane == HEAD_DIM + 4, c_mid, c_lo))
        k_bias = jnp.where(lane < HEAD_DIM + 3, 1.0, jnp.where(lane < HEAD_DIM + 6, parts, 0.0))
        v_one = jnp.where(lane == HEAD_DIM, 1.0, 0.0)
        q_ref[0, h] = jnp.where(lo, take(q), q_bias).astype(jnp.bfloat16)
        k_ref[0, h] = jnp.where(lo, take(k), k_bias).astype(jnp.bfloat16)
        v_ref[0, h] = jnp.where(lo, take(v), v_one).astype(jnp.bfloat16)


def _inproj_call(x, mod3, ln1_g, bf_pad, w_in_p, rwkv_params, tm):
    bsz, seq, _ = x.shape
    grid = (bsz, seq // tm)
    nck = tm // CHUNK
    tok = lambda n: pl.BlockSpec((1, tm, n), lambda b, s: (b, s, 0))
    const = lambda shape: pl.BlockSpec(shape, lambda b, s: (0,) * len(shape))
    head = pl.BlockSpec((1, HEADS, tm, LANES), lambda b, s: (b, 0, s, 0))
    out_shapes = (
        jax.ShapeDtypeStruct((bsz, HEADS, seq, LANES), jnp.bfloat16),
        jax.ShapeDtypeStruct((bsz, HEADS, seq, LANES), jnp.bfloat16),
        jax.ShapeDtypeStruct((bsz, HEADS, seq, LANES), jnp.bfloat16),
        jax.ShapeDtypeStruct((bsz, seq, 2 * D_MODEL), jnp.bfloat16),
        jax.ShapeDtypeStruct((bsz, seq, 2 * WIDTH), jnp.bfloat16),
        jax.ShapeDtypeStruct((bsz, seq // CHUNK, WIDTH, 2 * CHUNK), jnp.bfloat16),
        jax.ShapeDtypeStruct((bsz, seq, WIDTH), jnp.bfloat16),
        jax.ShapeDtypeStruct((bsz, seq, 2 * WIDTH), jnp.float32),
        jax.ShapeDtypeStruct((bsz, seq // CHUNK, 1, WIDTH), jnp.float32),
    )
    return pl.pallas_call(
        _inproj_kernel,
        grid=grid,
        in_specs=[tok(D_MODEL),
                  pl.BlockSpec((1, 1, 6 * D_MODEL), lambda b, s: (b, 0, 0)),
                  const((1, D_MODEL)), const((1, LANES)), const((D_MODEL, IN_P))]
                 + [const(t.shape) for t in rwkv_params],
        out_specs=(head, head, head, tok(2 * D_MODEL), tok(2 * WIDTH),
                   pl.BlockSpec((1, nck, WIDTH, 2 * CHUNK), lambda b, s: (b, s, 0, 0)),
                   tok(WIDTH), tok(2 * WIDTH),
                   pl.BlockSpec((1, nck, 1, WIDTH), lambda b, s: (b, s, 0, 0))),
        out_shape=out_shapes,
        scratch_shapes=[pltpu.VMEM((1, LANES), jnp.float32),
                        pltpu.VMEM((1, 4 * WIDTH), jnp.float32)],
        compiler_params=pltpu.CompilerParams(
            dimension_semantics=("arbitrary", "arbitrary"), vmem_limit_bytes=VMEM_LIMIT),
        name="in_proj",
    )(x, mod3, ln1_g, bf_pad, w_in_p, *rwkv_params)


NEG = -1e30


def _fox_kernel(q_ref, k_ref, v_ref, o_ref, sa_ref, sb_ref, acc_ref, *, tq):
    nt = q_ref.shape[2] // tq
    n_items = nt * (nt + 1) // 2
    causal = _tri(tq, strict=False)

    def logits(i, j, s_ref):
        i0 = pl.multiple_of(i * tq, tq)
        j0 = pl.multiple_of(j * tq, tq)
        for hh in range(2):
            s_ref[hh] = lax.dot_general(q_ref[0, hh, pl.ds(i0, tq), :], k_ref[0, hh, pl.ds(j0, tq), :], NT,
                                        preferred_element_type=jnp.float32)

    def advance(i, j):
        inner = j < i
        return jnp.minimum(jnp.where(inner, i, i + 1), nt - 1), jnp.where(inner, j + 1, 0)

    def item(i, j, s_ref, carry):
        j0 = pl.multiple_of(j * tq, tq)
        first = j == 0
        keep = jnp.logical_or(causal, j < i)
        new = []
        for hh in range(2):
            m, acc = carry[hh]
            m = jnp.where(first, NEG, m)
            acc = jnp.where(first, 0.0, acc)
            sc = jnp.where(keep, s_ref[hh], NEG)
            m_new = jnp.maximum(m, jnp.max(sc, axis=-1, keepdims=True))
            p = jnp.exp(sc - m_new).astype(jnp.bfloat16)
            acc = jnp.exp(m - m_new) * acc + jnp.dot(p, v_ref[0, hh, pl.ds(j0, tq), :],
                                                     preferred_element_type=jnp.float32)
            acc_ref[i, hh] = acc
            new.append((m_new, acc))
        return tuple(new)

    per_trip = 4 if n_items % 4 == 0 else 2
    bufs = (sa_ref, sb_ref)

    def trip(t, state):
        i, j, carry = state
        for u in range(per_trip):
            i_next, j_next = advance(i, j)
            logits(i_next, j_next, bufs[(u + 1) % 2])
            carry = item(i, j, bufs[u % 2], carry)
            i, j = i_next, j_next
        return i, j, carry

    init = tuple((jnp.full((tq, 1), NEG, jnp.float32), jnp.zeros((tq, LANES), jnp.float32))
                 for _ in range(2))
    zero = jnp.zeros((), jnp.int32)
    logits(zero, zero, sa_ref)
    lax.fori_loop(0, n_items // per_trip, trip, (zero, zero, init))

    lane = lax.broadcasted_iota(jnp.int32, (tq, LANES), 1)
    for i in range(nt):
        acc0 = acc_ref[i, 0]
        acc1 = acc_ref[i, 1]
        o0 = acc0 / acc0[:, HEAD_DIM:HEAD_DIM + 1]
        o1 = pltpu.roll(acc1 / acc1[:, HEAD_DIM:HEAD_DIM + 1], HEAD_DIM, 1)
        o_ref[0, i * tq:(i + 1) * tq, :] = jnp.where(lane < HEAD_DIM, o0, o1).astype(o_ref.dtype)


def _fox_call(q_aug, k_aug, v_aug, tq):
    bsz, _, seq, _ = q_aug.shape
    npair = HEADS // 2
    nt = seq // tq
    assert (nt * (nt + 1) // 2) % 2 == 0, "the flat (tile, block) loop runs two items per trip"
    pair_spec = pl.BlockSpec((1, 2, seq, LANES), lambda b, p: (b, p, 0, 0))
    return pl.pallas_call(
        functools.partial(_fox_kernel, tq=tq),
        grid=(bsz, npair),
        in_specs=[pair_spec, pair_spec, pair_spec],
        out_specs=pl.BlockSpec((1, seq, LANES), lambda b, p: (b, 0, p)),
        out_shape=jax.ShapeDtypeStruct((bsz, seq, WIDTH), jnp.bfloat16),
        scratch_shapes=[pltpu.VMEM((2, tq, tq), jnp.float32), pltpu.VMEM((2, tq, tq), jnp.float32),
                        pltpu.VMEM((nt, 2, tq, LANES), jnp.float32)],
        compiler_params=pltpu.CompilerParams(
            dimension_semantics=("arbitrary", "arbitrary"), vmem_limit_bytes=VMEM_LIMIT),
        name="fox_attn",
    )(q_aug, k_aug, v_aug)


BMM = (((2,), (1,)), ((0,), (0,)))


def _bmm(a, b):
    return lax.dot_general(a.astype(jnp.bfloat16), b.astype(jnp.bfloat16), BMM,
                           preferred_element_type=jnp.float32)


def _rwkv_kernel(lhs_in, rhst_in, vv_in, gb_in, wc_in, lnw_ref, lnb_ref, o_ref,
                 st_ref, lhs_ref, v_ref, wc_ref, y_ref):
    s = pl.program_id(1)
    tm = lhs_in.shape[1]
    nchunk = tm // CHUNK
    nb = nchunk * HEADS

    @pl.when(s == 0)
    def _():
        st_ref[...] = jnp.zeros_like(st_ref)

    for c in range(nchunk):
        sl = slice(c * CHUNK, (c + 1) * CHUNK)
        lhs_c = lhs_in[0, sl, :]
        vc = vv_in[0, sl, :]
        wct = jnp.broadcast_to(wc_in[0, c], (8, WIDTH)).T[:, 0:1]
        for h in range(HEADS):
            i = c * HEADS + h
            hs = slice(h * HEAD_DIM, (h + 1) * HEAD_DIM)
            lhs_ref[i, 0:CHUNK, :] = lhs_c[:, hs]
            lhs_ref[i, CHUNK:, :] = lhs_c[:, WIDTH + h * HEAD_DIM:WIDTH + (h + 1) * HEAD_DIM]
            v_ref[i] = vc[:, hs]
            wc_ref[i] = jnp.broadcast_to(wct[hs, :], (HEAD_DIM, HEAD_DIM))
    rhst = rhst_in[0].reshape(nb, HEAD_DIM, 2 * CHUNK)
    g = gb_in[0, :, 0:WIDTH]
    bonus = gb_in[0, :, WIDTH:]

    r2 = lax.broadcasted_iota(jnp.int32, (2 * CHUNK, 2 * CHUNK), 0)
    c2 = lax.broadcasted_iota(jnp.int32, (2 * CHUNK, 2 * CHUNK), 1)
    tt = r2 & (CHUNK - 1)
    jj = c2 & (CHUNK - 1)
    aa_mask = (tt > jj) | ((r2 >= CHUNK) & (tt == jj))
    eye = (_tri(CHUNK, False) & ~_tri(CHUNK, True)).astype(jnp.float32)
    ri = lax.broadcasted_iota(jnp.int32, (CHUNK, CHUNK), 0)
    ci = lax.broadcasted_iota(jnp.int32, (CHUNK, CHUNK), 1)
    lvl_masks = [((ri >> (l + 1)) == (ci >> (l + 1))) & ((ri >> l) != (ci >> l))
                 for l in range(CHUNK.bit_length() - 1)]

    lhs = lhs_ref[...]
    aa = jnp.where(aa_mask, _bmm(lhs, rhst), 0.0)
    a_ab = aa[:, 0:CHUNK, 0:CHUNK]
    a_ak = aa[:, 0:CHUNK, CHUNK:]
    a_r = aa[:, CHUNK:, :]
    t = eye + jnp.where(lvl_masks[0], a_ab, 0.0)
    for lm in lvl_masks[1:]:
        t = t + _bmm(t, _bmm(jnp.where(lm, a_ab, 0.0), t))
    vb = v_ref[...]
    x0 = _bmm(a_ak, vb)
    tt = _bmm(t, jnp.concatenate([lhs[:, 0:CHUNK, :], x0.astype(jnp.bfloat16)], axis=-1))
    ta = tt[:, :, 0:HEAD_DIM]
    tx = tt[:, :, HEAD_DIM:]
    yl = jnp.concatenate([lhs[:, CHUNK:, :], a_r.astype(jnp.bfloat16)], axis=-1)

    st = st_ref[...]
    for c in range(nchunk):
        blk = slice(c * HEADS, (c + 1) * HEADS)
        sl = slice(c * CHUNK, (c + 1) * CHUNK)
        stb = st.astype(jnp.bfloat16)
        u = (_bmm(ta[blk], stb) + tx[blk]).astype(jnp.bfloat16)
        uv = jnp.concatenate([u, vb[blk]], axis=1)
        st = wc_ref[blk] * (st + _bmm(rhst[blk], uv))
        y = _bmm(yl[blk], jnp.concatenate([stb, uv], axis=1))
        for h in range(HEADS):
            y_ref[c, :, h * HEAD_DIM:(h + 1) * HEAD_DIM] = y[h]
        yv = y_ref[c]
        mean = _head_sum(yv) * (1.0 / HEAD_DIM)
        d = yv - mean
        var = _head_sum(d * d) * (1.0 / HEAD_DIM)
        yn = d * lax.rsqrt(var + GN_EPS) * lnw_ref[...] + lnb_ref[...]
        o_ref[0, sl, :] = ((yn + bonus[sl]) * g[sl]).astype(o_ref.dtype)
    st_ref[...] = st


def _rwkv_call(lhs, rhst, vv, gb, wc, ln_w, ln_b, tm):
    bsz, seq, _ = lhs.shape
    nchunk = tm // CHUNK
    nb = nchunk * HEADS
    tok = lambda n: pl.BlockSpec((1, tm, n), lambda b, s: (b, s, 0))
    row = lambda n: pl.BlockSpec((1, n), lambda b, s: (0, 0))
    return pl.pallas_call(
        _rwkv_kernel,
        grid=(bsz, seq // tm),
        in_specs=[tok(2 * WIDTH),
                  pl.BlockSpec((1, nchunk, WIDTH, 2 * CHUNK), lambda b, s: (b, s, 0, 0)),
                  tok(WIDTH), tok(2 * WIDTH),
                  pl.BlockSpec((1, nchunk, 1, WIDTH), lambda b, s: (b, s, 0, 0)),
                  row(WIDTH), row(WIDTH)],
        out_specs=tok(WIDTH),
        out_shape=jax.ShapeDtypeStruct((bsz, seq, WIDTH), jnp.bfloat16),
        scratch_shapes=[pltpu.VMEM((HEADS, HEAD_DIM, HEAD_DIM), jnp.float32),
                        pltpu.VMEM((nb, 2 * CHUNK, HEAD_DIM), jnp.bfloat16),
                        pltpu.VMEM((nb, CHUNK, HEAD_DIM), jnp.bfloat16),
                        pltpu.VMEM((nb, HEAD_DIM, HEAD_DIM), jnp.float32),
                        pltpu.VMEM((nchunk, CHUNK, WIDTH), jnp.float32)],
        compiler_params=pltpu.CompilerParams(
            dimension_semantics=("arbitrary", "arbitrary"), vmem_limit_bytes=VMEM_LIMIT),
        name="rwkv7",
    )(lhs, rhst, vv, gb, wc, ln_w, ln_b)


FF_CHUNK = 256


def _post_kernel(of_ref, or_ref, gate_ref, x_ref, mod_ref, wfo_ref, wro_ref, wo_ref, g2_ref,
                 wup_ref, cw_ref, cb_ref, wdn_ref, fg_ref, o_ref, carry_ref):
    s = pl.program_id(1)
    tm = x_ref.shape[1]

    @pl.when(s == 0)
    def _():
        carry_ref[...] = jnp.zeros_like(carry_ref)

    mod = mod_ref[0]
    gate1 = mod[:, 2 * D_MODEL:3 * D_MODEL]
    shift2 = mod[:, 3 * D_MODEL:4 * D_MODEL]
    scale2 = mod[:, 4 * D_MODEL:5 * D_MODEL]
    gate2 = mod[:, 5 * D_MODEL:6 * D_MODEL]

    gate = gate_ref[0]
    m_fox = jnp.dot(of_ref[0], wfo_ref[...], preferred_element_type=jnp.float32)
    m_rwkv = jnp.dot(or_ref[0], wro_ref[...], preferred_element_type=jnp.float32)
    merged = gate[:, 0:D_MODEL] * m_fox + gate[:, D_MODEL:] * m_rwkv
    mix = jnp.dot(merged.astype(jnp.bfloat16), wo_ref[...], preferred_element_type=jnp.float32)
    h1 = x_ref[0] + gate1 * mix

    y = h1 * lax.rsqrt(jnp.mean(h1 * h1, axis=-1, keepdims=True) + RMS_EPS) * g2_ref[...]
    u2 = (y * (1.0 + scale2) + shift2).astype(jnp.bfloat16)

    row = lax.broadcasted_iota(jnp.int32, (tm, FF_CHUNK), 0)
    def up(c):
        lo = c * FF_CHUNK
        return (jnp.dot(u2, wup_ref[:, lo:lo + FF_CHUNK], preferred_element_type=jnp.float32),
                jnp.dot(u2, wup_ref[:, D_FF + lo:D_FF + lo + FF_CHUNK], preferred_element_type=jnp.float32))

    acc = jnp.zeros((tm, D_MODEL), jnp.float32)
    nxt = up(0)
    for c in range(D_FF // FF_CHUNK):
        cs = slice(c * FF_CHUNK, (c + 1) * FF_CHUNK)
        a, b = nxt
        if c + 1 < D_FF // FF_CHUNK:
            nxt = up(c + 1)
        tail = carry_ref[:, cs]
        p1 = tail[7:8, :]
        p2 = tail[6:7, :]
        a1 = jnp.where(row == 0, p1, pltpu.roll(a, 1, 0))
        a2 = jnp.where(row == 0, p2, jnp.where(row == 1, p1, pltpu.roll(a, 2, 0)))
        carry_ref[:, cs] = a[tm - 8:tm, :]
        cw = cw_ref[:, cs]
        conv = cw[0:1, :] * a2 + cw[1:2, :] * a1 + cw[2:3, :] * a + cb_ref[:, cs]
        act = conv * _sigmoid(conv) * b
        acc = acc + jnp.dot(act.astype(jnp.bfloat16), wdn_ref[cs, :], preferred_element_type=jnp.float32)

    h2 = h1 + gate2 * acc
    o_ref[0] = h2 * lax.rsqrt(jnp.mean(h2 * h2, axis=-1, keepdims=True) + RMS_EPS) * fg_ref[...]


def _post_call(o_fox, o_rwkv, gates, x, mod3, wfo, wro, wo, ln2_g, wup, conv_w, conv_b, wdn, final_g, tm):
    bsz, seq, _ = x.shape
    tok = lambda n: pl.BlockSpec((1, tm, n), lambda b, s: (b, s, 0))
    const = lambda shape: pl.BlockSpec(shape, lambda b, s: (0,) * len(shape),
                                       pipeline_mode=pl.Buffered(1))
    return pl.pallas_call(
        _post_kernel,
        grid=(bsz, seq // tm),
        in_specs=[tok(WIDTH), tok(WIDTH), tok(2 * D_MODEL), tok(D_MODEL),
                  pl.BlockSpec((1, 1, 6 * D_MODEL), lambda b, s: (b, 0, 0)),
                  const((WIDTH, D_MODEL)), const((WIDTH, D_MODEL)), const((D_MODEL, D_MODEL)),
                  const((1, D_MODEL)), const((D_MODEL, 2 * D_FF)), const((3, D_FF)), const((1, D_FF)),
                  const((D_FF, D_MODEL)), const((1, D_MODEL))],
        out_specs=tok(D_MODEL),
        out_shape=jax.ShapeDtypeStruct((bsz, seq, D_MODEL), jnp.float32),
        scratch_shapes=[pltpu.VMEM((8, D_FF), jnp.float32)],
        compiler_params=pltpu.CompilerParams(
            dimension_semantics=("arbitrary", "arbitrary"), vmem_limit_bytes=VMEM_LIMIT),
        name="post_ffn",
    )(o_fox, o_rwkv, gates, x, mod3, wfo, wro, wo, ln2_g, wup, conv_w, conv_b, wdn, final_g)


def _pad_cols(w, n):
    return jnp.pad(w, ((0, 0), (0, n - w.shape[1])))


def _layer(h, c, w_mod, b_mod, ln1_g, w_in, fox_b_f, fox_w_out, rwkv_mu, rwkv_w0, rwkv_w2, rwkv_a0,
           rwkv_a2, rwkv_g2, rwkv_k_k, rwkv_k_a, rwkv_r_k, rwkv_ln_w, rwkv_ln_b, rwkv_w_out, w_o,
           ln2_g, w_up, conv_w, conv_b, w_down, out_gain):
    bsz, seq, _ = h.shape
    bf16 = jnp.bfloat16
    lora = (LORA_PAD - 64)

    fox_cols = 3 * WIDTH + HEADS
    o_r = fox_cols
    o_g = o_r + 3 * WIDTH + 64 + 64 + GATE_LORA
    w_in_p = jnp.concatenate([
        w_in[:, 0:3 * WIDTH],
        w_in[:, o_r:o_r + 3 * WIDTH],
        _pad_cols(w_in[:, o_r + 3 * WIDTH:o_r + 3 * WIDTH + 64], LORA_PAD),
        _pad_cols(w_in[:, o_r + 3 * WIDTH + 64:o_r + 3 * WIDTH + 128], LORA_PAD),
        w_in[:, o_r + 3 * WIDTH + 128:o_g],
        _pad_cols(w_in[:, 3 * WIDTH:fox_cols], LANES),
        w_in[:, o_g:]], axis=1).astype(bf16)
    mu = rwkv_mu.reshape(1, -1)
    mu_p = jnp.concatenate([mu[:, 0:3 * WIDTH], _pad_cols(mu[:, 3 * WIDTH:3 * WIDTH + 64], LORA_PAD),
                            _pad_cols(mu[:, 3 * WIDTH + 64:3 * WIDTH + 128], LORA_PAD),
                            mu[:, 3 * WIDTH + 128:], jnp.zeros((1, LANES), mu.dtype)], axis=1)
    w2_p = jnp.pad(rwkv_w2, ((0, lora), (0, 0))).astype(bf16)
    a2_p = jnp.pad(rwkv_a2, ((0, lora), (0, 0))).astype(bf16)
    bf_pad = _pad_cols(fox_b_f.reshape(1, HEADS), LANES)
    rowv = lambda t: t.reshape(1, -1)

    tm_in = min(512, seq)
    tq = min(512, seq)
    tm_rw = min(256, seq)
    tm_post = min(512, seq)

    mod = _mod_call(c, w_mod, b_mod)
    mod3 = mod.reshape(bsz, 1, 6 * D_MODEL)
    rwkv_params = (mu_p, rowv(rwkv_w0), w2_p, rowv(rwkv_a0), a2_p, rwkv_g2.astype(bf16),
                   rowv(rwkv_k_k), rowv(rwkv_k_a), rowv(rwkv_r_k))
    q_aug, k_aug, v_aug, gates, r_lhs, r_rhst, r_v, r_gb, r_wc = _inproj_call(
        h, mod3, rowv(ln1_g), bf_pad, w_in_p, rwkv_params, tm_in)
    o_fox = _fox_call(q_aug, k_aug, v_aug, tq)
    o_rwkv = _rwkv_call(r_lhs, r_rhst, r_v, r_gb, r_wc, rowv(rwkv_ln_w), rowv(rwkv_ln_b), tm_rw)
    return _post_call(o_fox, o_rwkv, gates, h, mod3, fox_w_out.astype(bf16), rwkv_w_out.astype(bf16),
                      w_o.astype(bf16), rowv(ln2_g), w_up.astype(bf16), conv_w, rowv(conv_b),
                      w_down.astype(bf16), rowv(out_gain), tm_post)


def kernel(x, c, w_mod, b_mod, ln1_g, w_in, fox_b_f, fox_w_out, rwkv_mu, rwkv_w0, rwkv_w2, rwkv_a0,
           rwkv_a2, rwkv_g2, rwkv_k_k, rwkv_k_a, rwkv_r_k, rwkv_ln_w, rwkv_ln_b, rwkv_w_out, w_o,
           ln2_g, w_up, conv_w, conv_b, w_down, final_g):
    depth = w_mod.shape[0]
    assert depth == 1, "single trunk layer: the final RMSNorm is fused into the layer's last kernel"
    return _layer(x, c, w_mod[0], b_mod[0], ln1_g[0], w_in[0], fox_b_f[0], fox_w_out[0], rwkv_mu[0],
                  rwkv_w0[0], rwkv_w2[0], rwkv_a0[0], rwkv_a2[0], rwkv_g2[0], rwkv_k_k[0], rwkv_k_a[0],
                  rwkv_r_k[0], rwkv_ln_w[0], rwkv_ln_b[0], rwkv_w_out[0], w_o[0], ln2_g[0], w_up[0],
                  conv_w[0], conv_b[0], w_down[0], final_g)
```

```python
import functools

import jax
import jax.numpy as jnp
from jax import lax
from jax.experimental import pallas as pl
from jax.experimental.pallas import tpu as pltpu

D_MODEL = 1024
HEADS = 8
HEAD_DIM = 64
WIDTH = HEADS * HEAD_DIM
LORA_PAD = 128
GATE_LORA = 128
D_FF = 2816
RMS_EPS = 1e-6
GN_EPS = HEAD_DIM * 1e-5
L2_EPS = 1e-12
LANES = 128
CHUNK = 64

OFF_QKV = 0
OFF_RWKV = 3 * WIDTH
RWKV_P = 3 * WIDTH + 2 * LORA_PAD + GATE_LORA
OFF_F = OFF_RWKV + RWKV_P
OFF_GATE = OFF_F + LANES
IN_P = OFF_GATE + 2 * D_MODEL

VMEM_LIMIT = 56 * 1024 * 1024

HI = lax.Precision.HIGHEST
NT = (((1,), (1,)), ((), ()))


def _bdot(a, b):
    return jnp.dot(a.astype(jnp.bfloat16), b.astype(jnp.bfloat16), preferred_element_type=jnp.float32)


def _sigmoid(x):
    return 1.0 / (1.0 + jnp.exp(-x))


def _softplus(x):
    return jnp.maximum(x, 0.0) + jnp.log(1.0 + jnp.exp(-jnp.abs(x)))


def _tri(n, strict):
    r = lax.broadcasted_iota(jnp.int32, (n, n), 0)
    c = lax.broadcasted_iota(jnp.int32, (n, n), 1)
    return (r > c) if strict else (r >= c)


def _head_sum(x):
    lane = lax.broadcasted_iota(jnp.int32, (x.shape[0], LANES), 1)
    lo = lane < HEAD_DIM
    outs = []
    for g in range(x.shape[1] // LANES):
        xg = x[:, g * LANES:(g + 1) * LANES]
        s0 = jnp.sum(jnp.where(lo, xg, 0.0), axis=-1, keepdims=True)
        s1 = jnp.sum(jnp.where(lo, 0.0, xg), axis=-1, keepdims=True)
        outs.append(jnp.where(lo, s0, s1))
    return jnp.concatenate(outs, axis=-1)


def _mod_kernel(c_ref, w_ref, b_ref, o_ref):
    c = c_ref[...]
    sc = c * _sigmoid(c)
    o_ref[...] = jnp.dot(sc, w_ref[...], precision=HI, preferred_element_type=jnp.float32) + b_ref[...]


def _mod_call(c, w_mod, b_mod):
    bsz = c.shape[0]
    n = w_mod.shape[1]
    tn = 1024
    return pl.pallas_call(
        _mod_kernel,
        grid=(n // tn,),
        in_specs=[pl.BlockSpec((bsz, D_MODEL), lambda j: (0, 0)),
                  pl.BlockSpec((D_MODEL, tn), lambda j: (0, j)),
                  pl.BlockSpec((1, tn), lambda j: (0, j))],
        out_specs=pl.BlockSpec((bsz, tn), lambda j: (0, j)),
        out_shape=jax.ShapeDtypeStruct((bsz, n), jnp.float32),
        name="mod",
    )(c, w_mod, b_mod.reshape(1, n))


def _split3(c):
    f32 = jnp.float32
    hi = c.astype(jnp.bfloat16).astype(f32)
    mid = (c - hi).astype(jnp.bfloat16).astype(f32)
    lo = (c - hi - mid).astype(jnp.bfloat16).astype(f32)
    return hi, mid, lo


def _cumsum_rows(tri, x):
    return sum(jnp.dot(tri, part.astype(jnp.bfloat16), preferred_element_type=jnp.float32)
               for part in _split3(x))


def _inproj_kernel(x_ref, mod_ref, g_ref, bf_ref, w_ref,
                   mu_ref, w0_ref, w2_ref, a0_ref, a2_ref, g2_ref, kk_ref, ka_ref, rk_ref,
                   q_ref, k_ref, v_ref, gate_ref, lhs_ref, rhst_ref, vv_ref, gb_ref, wc_ref,
                   carry_ref, pcarry_ref):
    s = pl.program_id(1)
    tm = x_ref.shape[1]

    @pl.when(s == 0)
    def _():
        carry_ref[...] = jnp.zeros_like(carry_ref)
        pcarry_ref[...] = jnp.zeros_like(pcarry_ref)

    x = x_ref[0]
    mod = mod_ref[0]
    shift = mod[:, 0:D_MODEL]
    scale = mod[:, D_MODEL:2 * D_MODEL]
    y = x * lax.rsqrt(jnp.mean(x * x, axis=-1, keepdims=True) + RMS_EPS) * g_ref[...]
    u = (y * (1.0 + scale) + shift).astype(jnp.bfloat16)

    def proj(off, n):
        return jnp.dot(u, w_ref[:, off:off + n], preferred_element_type=jnp.float32)

    row = lax.broadcasted_iota(jnp.int32, (tm, WIDTH), 0)

    def shifted(off):
        cur = proj(OFF_RWKV + off, WIDTH)
        prev = jnp.where(row == 0, pcarry_ref[:, off:off + WIDTH], pltpu.roll(cur, 1, 0))
        pcarry_ref[:, off:off + WIDTH] = cur[tm - 1:tm, :]
        return cur, cur + (prev - cur) * mu_ref[:, off:off + WIDTH]

    def gate(i):
        gate_ref[0, :, i * 512:(i + 1) * 512] = proj(OFF_GATE + i * 512, 512).astype(jnp.bfloat16)

    _, r = shifted(0)
    _, k_r = shifted(WIDTH)
    _, v_r = shifted(2 * WIDTH)
    tail, tail_l = shifted(3 * WIDTH)
    xw = tail_l[:, 0:LORA_PAD]
    xa = tail_l[:, LORA_PAD:2 * LORA_PAD]
    xg = tail_l[:, 2 * LORA_PAD:2 * LORA_PAD + GATE_LORA]
    gate(0)
    zw = w0_ref[...] + _bdot(jnp.tanh(xw), w2_ref[...])
    lw = -jnp.exp(-_softplus(-zw) - 0.5)
    al = _sigmoid(a0_ref[...] + _bdot(xa, a2_ref[...]))
    gb_ref[0, :, 0:WIDTH] = _bdot(_sigmoid(xg), g2_ref[...])
    gate(1)
    kk = k_r * kk_ref[...]
    kk = kk / jnp.maximum(jnp.sqrt(_head_sum(kk * kk)), L2_EPS)
    gate(2)
    k_mod = k_r * (1.0 + (al - 1.0) * ka_ref[...])
    b_vec = kk * al
    gb_ref[0, :, WIDTH:] = _head_sum(r * k_mod * rk_ref[...]) * v_r
    vv_ref[0] = v_r.astype(jnp.bfloat16)

    z = tail[:, RWKV_P - 3 * WIDTH:] + bf_ref[...]
    logf = jnp.minimum(z, 0.0) - jnp.log(1.0 + jnp.exp(-jnp.abs(z)))
    cum = _cumsum_rows(_tri(tm, strict=False).astype(jnp.bfloat16), logf) + carry_ref[...]
    carry_ref[...] = cum[tm - 1:tm, :]

    tri = _tri(CHUNK, strict=False).astype(jnp.bfloat16)

    def chunk(c):
        sl = slice(c * CHUNK, (c + 1) * CHUNK)
        lwc = lw[sl]
        cl = _cumsum_rows(tri, lwc)
        el = jnp.exp(cl)
        inv = jnp.exp(-cl)
        lhs_ref[0, sl, 0:WIDTH] = (-kk[sl] * jnp.exp(cl - lwc)).astype(jnp.bfloat16)
        lhs_ref[0, sl, WIDTH:] = (r[sl] * el).astype(jnp.bfloat16)
        rhs = jnp.concatenate([b_vec[sl] * inv, k_mod[sl] * inv], axis=0)
        rhst_ref[0, c] = rhs.T.astype(jnp.bfloat16)
        wc_ref[0, c] = el[CHUNK - 1:CHUNK, :]

    nck = tm // CHUNK
    cuts = [0, (nck + 2) // 3, (2 * nck + 2) // 3, nck]
    q = proj(OFF_QKV, WIDTH) * (HEAD_DIM ** -0.5)
    for c in range(cuts[0], cuts[1]):
        chunk(c)
    k = proj(OFF_QKV + WIDTH, WIDTH)
    for c in range(cuts[1], cuts[2]):
        chunk(c)
    v = proj(OFF_QKV + 2 * WIDTH, WIDTH)
    for c in range(cuts[2], cuts[3]):
        chunk(c)
    gate(3)
    lane = lax.broadcasted_iota(jnp.int32, (tm, LANES), 1)
    lo = lane < HEAD_DIM
    for h in range(HEADS):
        g = slice((h // 2) * LANES, (h // 2 + 1) * LANES)
        take = (lambda t: t[:, g]) if h % 2 == 0 else (lambda t: pltpu.roll(t[:, g], HEAD_DIM, 1))
        c_hi, c_mid, c_lo = _split3(cum[:, h:h + 1])
        parts = jnp.where(lane == HEAD_DIM, c_hi, jnp.where(lane == HEAD_DIM + 1, c_mid, c_lo))
        q_bias = jnp.where(lane < HEAD_DIM + 3, parts, jnp.where(lane < HEAD_DIM + 6, -1.0, 0.0))
        parts = jnp.where(lane == HEAD_DIM + 3, c_hi, jnp.where(lane == HEAD_DIM + 4, c_mid, c_lo))
        k_bias = jnp.where(lane < HEAD_DIM + 3, 1.0, jnp.where(lane < HEAD_DIM + 6, parts, 0.0))
        v_one = jnp.where(lane == HEAD_DIM, 1.0, 0.0)
        q_ref[0, h] = jnp.where(lo, take(q), q_bias).astype(jnp.bfloat16)
        k_ref[0, h] = jnp.where(lo, take(k), k_bias).astype(jnp.bfloat16)
        v_ref[0, h] = jnp.where(lo, take(v), v_one).astype(jnp.bfloat16)


def _inproj_call(x, mod3, ln1_g, bf_pad, w_in_p, rwkv_params, tm):
    bsz, seq, _ = x.shape
    grid = (bsz, seq // tm)
    nck = tm // CHUNK
    tok = lambda n: pl.BlockSpec((1, tm, n), lambda b, s: (b, s, 0))
    const = lambda shape: pl.BlockSpec(shape, lambda b, s: (0,) * len(shape))
    head = pl.BlockSpec((1, HEADS, tm, LANES), lambda b, s: (b, 0, s, 0))
    out_shapes = (
        jax.ShapeDtypeStruct((bsz, HEADS, seq, LANES), jnp.bfloat16),
        jax.ShapeDtypeStruct((bsz, HEADS, seq, LANES), jnp.bfloat16),
        jax.ShapeDtypeStruct((bsz, HEADS, seq, LANES), jnp.bfloat16),
        jax.ShapeDtypeStruct((bsz, seq, 2 * D_MODEL), jnp.bfloat16),
        jax.ShapeDtypeStruct((bsz, seq, 2 * WIDTH), jnp.bfloat16),
        jax.ShapeDtypeStruct((bsz, seq // CHUNK, WIDTH, 2 * CHUNK), jnp.bfloat16),
        jax.ShapeDtypeStruct((bsz, seq, WIDTH), jnp.bfloat16),
        jax.ShapeDtypeStruct((bsz, seq, 2 * WIDTH), jnp.float32),
        jax.ShapeDtypeStruct((bsz, seq // CHUNK, 1, WIDTH), jnp.float32),
    )
    return pl.pallas_call(
        _inproj_kernel,
        grid=grid,
        in_specs=[tok(D_MODEL),
                  pl.BlockSpec((1, 1, 6 * D_MODEL), lambda b, s: (b, 0, 0)),
                  const((1, D_MODEL)), const((1, LANES)), const((D_MODEL, IN_P))]
                 + [const(t.shape) for t in rwkv_params],
        out_specs=(head, head, head, tok(2 * D_MODEL), tok(2 * WIDTH),
                   pl.BlockSpec((1, nck, WIDTH, 2 * CHUNK), lambda b, s: (b, s, 0, 0)),
                   tok(WIDTH), tok(2 * WIDTH),
                   pl.BlockSpec((1, nck, 1, WIDTH), lambda b, s: (b, s, 0, 0))),
        out_shape=out_shapes,
        scratch_shapes=[pltpu.VMEM((1, LANES), jnp.float32),
                        pltpu.VMEM((1, 4 * WIDTH), jnp.float32)],
        compiler_params=pltpu.CompilerParams(
            dimension_semantics=("arbitrary", "arbitrary"), vmem_limit_bytes=VMEM_LIMIT),
        name="in_proj",
    )(x, mod3, ln1_g, bf_pad, w_in_p, *rwkv_params)


NEG = -1e30


def _fox_kernel(q_ref, k_ref, v_ref, o_ref, sa_ref, sb_ref, acc_ref, *, tq):
    nt = q_ref.shape[2] // tq
    n_items = nt * (nt + 1) // 2
    causal = _tri(tq, strict=False)

    def logits(i, j, s_ref):
        i0 = pl.multiple_of(i * tq, tq)
        j0 = pl.multiple_of(j * tq, tq)
        for hh in range(2):
            s_ref[hh] = lax.dot_general(q_ref[0, hh, pl.ds(i0, tq), :], k_ref[0, hh, pl.ds(j0, tq), :], NT,
                                        preferred_element_type=jnp.float32)

    def advance(i, j):
        inner = j < i
        return jnp.minimum(jnp.where(inner, i, i + 1), nt - 1), jnp.where(inner, j + 1, 0)

    def item(i, j, s_ref, carry):
        j0 = pl.multiple_of(j * tq, tq)
        first = j == 0
        keep = jnp.logical_or(causal, j < i)
        new = []
        for hh in range(2):
            m, acc = carry[hh]
            m = jnp.where(first, NEG, m)
            acc = jnp.where(first, 0.0, acc)
            sc = jnp.where(keep, s_ref[hh], NEG)
            m_new = jnp.maximum(m, jnp.max(sc, axis=-1, keepdims=True))
            p = jnp.exp(sc - m_new).astype(jnp.bfloat16)
            acc = jnp.exp(m - m_new) * acc + jnp.dot(p, v_ref[0, hh, pl.ds(j0, tq), :],
                                                     preferred_element_type=jnp.float32)
            acc_ref[i, hh] = acc
            new.append((m_new, acc))
        return tuple(new)

    per_trip = next(u for u in (6, 4, 2) if n_items % u == 0)
    bufs = (sa_ref, sb_ref)

    def trip(t, state):
        i, j, carry = state
        for u in range(per_trip):
            i_next, j_next = advance(i, j)
            logits(i_next, j_next, bufs[(u + 1) % 2])
            carry = item(i, j, bufs[u % 2], carry)
            i, j = i_next, j_next
        return i, j, carry

    init = tuple((jnp.full((tq, 1), NEG, jnp.float32), jnp.zeros((tq, LANES), jnp.float32))
                 for _ in range(2))
    zero = jnp.zeros((), jnp.int32)
    logits(zero, zero, sa_ref)
    lax.fori_loop(0, n_items // per_trip, trip, (zero, zero, init))

    lane = lax.broadcasted_iota(jnp.int32, (tq, LANES), 1)
    for i in range(nt):
        acc0 = acc_ref[i, 0]
        acc1 = acc_ref[i, 1]
        o0 = acc0 / acc0[:, HEAD_DIM:HEAD_DIM + 1]
        o1 = pltpu.roll(acc1 / acc1[:, HEAD_DIM:HEAD_DIM + 1], HEAD_DIM, 1)
        o_ref[0, i * tq:(i + 1) * tq, :] = jnp.where(lane < HEAD_DIM, o0, o1).astype(o_ref.dtype)


def _fox_call(q_aug, k_aug, v_aug, tq):
    bsz, _, seq, _ = q_aug.shape
    npair = HEADS // 2
    nt = seq // tq
    assert (nt * (nt + 1) // 2) % 2 == 0, "the flat (tile, block) loop runs two items per trip"
    pair_spec = pl.BlockSpec((1, 2, seq, LANES), lambda b, p: (b, p, 0, 0))
    return pl.pallas_call(
        functools.partial(_fox_kernel, tq=tq),
        grid=(bsz, npair),
        in_specs=[pair_spec, pair_spec, pair_spec],
        out_specs=pl.BlockSpec((1, seq, LANES), lambda b, p: (b, 0, p)),
        out_shape=jax.ShapeDtypeStruct((bsz, seq, WIDTH), jnp.bfloat16),
        scratch_shapes=[pltpu.VMEM((2, tq, tq), jnp.float32), pltpu.VMEM((2, tq, tq), jnp.float32),
                        pltpu.VMEM((nt, 2, tq, LANES), jnp.float32)],
        compiler_params=pltpu.CompilerParams(
            dimension_semantics=("arbitrary", "arbitrary"), vmem_limit_bytes=VMEM_LIMIT),
        name="fox_attn",
    )(q_aug, k_aug, v_aug)


BMM = (((2,), (1,)), ((0,), (0,)))


def _bmm(a, b):
    return lax.dot_general(a.astype(jnp.bfloat16), b.astype(jnp.bfloat16), BMM,
                           preferred_element_type=jnp.float32)


def _rwkv_kernel(lhs_in, rhst_in, vv_in, gb_in, wc_in, lnw_ref, lnb_ref, o_ref,
                 st_ref, lhs_ref, v_ref, wc_ref, y_ref):
    s = pl.program_id(1)
    tm = lhs_in.shape[1]
    nchunk = tm // CHUNK
    nb = nchunk * HEADS

    @pl.when(s == 0)
    def _():
        st_ref[...] = jnp.zeros_like(st_ref)

    for c in range(nchunk):
        sl = slice(c * CHUNK, (c + 1) * CHUNK)
        lhs_c = lhs_in[0, sl, :]
        vc = vv_in[0, sl, :]
        wct = jnp.broadcast_to(wc_in[0, c], (8, WIDTH)).T[:, 0:1]
        for h in range(HEADS):
            i = c * HEADS + h
            hs = slice(h * HEAD_DIM, (h + 1) * HEAD_DIM)
            lhs_ref[i, 0:CHUNK, :] = lhs_c[:, hs]
            lhs_ref[i, CHUNK:, :] = lhs_c[:, WIDTH + h * HEAD_DIM:WIDTH + (h + 1) * HEAD_DIM]
            v_ref[i] = vc[:, hs]
            wc_ref[i] = jnp.broadcast_to(wct[hs, :], (HEAD_DIM, HEAD_DIM))
    rhst = rhst_in[0].reshape(nb, HEAD_DIM, 2 * CHUNK)
    g = gb_in[0, :, 0:WIDTH]
    bonus = gb_in[0, :, WIDTH:]

    r2 = lax.broadcasted_iota(jnp.int32, (2 * CHUNK, 2 * CHUNK), 0)
    c2 = lax.broadcasted_iota(jnp.int32, (2 * CHUNK, 2 * CHUNK), 1)
    tt = r2 & (CHUNK - 1)
    jj = c2 & (CHUNK - 1)
    aa_mask = (tt > jj) | ((r2 >= CHUNK) & (tt == jj))
    eye = (_tri(CHUNK, False) & ~_tri(CHUNK, True)).astype(jnp.float32)
    ri = lax.broadcasted_iota(jnp.int32, (CHUNK, CHUNK), 0)
    ci = lax.broadcasted_iota(jnp.int32, (CHUNK, CHUNK), 1)
    lvl_masks = [((ri >> (l + 1)) == (ci >> (l + 1))) & ((ri >> l) != (ci >> l))
                 for l in range(CHUNK.bit_length() - 1)]

    lhs = lhs_ref[...]
    aa = jnp.where(aa_mask, _bmm(lhs, rhst), 0.0)
    a_ab = aa[:, 0:CHUNK, 0:CHUNK]
    a_ak = aa[:, 0:CHUNK, CHUNK:]
    a_r = aa[:, CHUNK:, :]
    t = eye + jnp.where(lvl_masks[0], a_ab, 0.0)
    for lm in lvl_masks[1:]:
        t = t + _bmm(t, _bmm(jnp.where(lm, a_ab, 0.0), t))
    vb = v_ref[...]
    x0 = _bmm(a_ak, vb)
    tt = _bmm(t, jnp.concatenate([lhs[:, 0:CHUNK, :], x0.astype(jnp.bfloat16)], axis=-1))
    ta = tt[:, :, 0:HEAD_DIM]
    tx = tt[:, :, HEAD_DIM:]
    yl = jnp.concatenate([lhs[:, CHUNK:, :], a_r.astype(jnp.bfloat16)], axis=-1)

    st = st_ref[...]
    for c in range(nchunk):
        blk = slice(c * HEADS, (c + 1) * HEADS)
        sl = slice(c * CHUNK, (c + 1) * CHUNK)
        stb = st.astype(jnp.bfloat16)
        u = (_bmm(ta[blk], stb) + tx[blk]).astype(jnp.bfloat16)
        uv = jnp.concatenate([u, vb[blk]], axis=1)
        st = wc_ref[blk] * (st + _bmm(rhst[blk], uv))
        y = _bmm(yl[blk], jnp.concatenate([stb, uv], axis=1))
        for h in range(HEADS):
            y_ref[c, :, h * HEAD_DIM:(h + 1) * HEAD_DIM] = y[h]
        yv = y_ref[c]
        mean = _head_sum(yv) * (1.0 / HEAD_DIM)
        d = yv - mean
        var = _head_sum(d * d) * (1.0 / HEAD_DIM)
        yn = d * lax.rsqrt(var + GN_EPS) * lnw_ref[...] + lnb_ref[...]
        o_ref[0, sl, :] = ((yn + bonus[sl]) * g[sl]).astype(o_ref.dtype)
    st_ref[...] = st


def _rwkv_call(lhs, rhst, vv, gb, wc, ln_w, ln_b, tm):
    bsz, seq, _ = lhs.shape
    nchunk = tm // CHUNK
    nb = nchunk * HEADS
    tok = lambda n: pl.BlockSpec((1, tm, n), lambda b, s: (b, s, 0))
    row = lambda n: pl.BlockSpec((1, n), lambda b, s: (0, 0))
    return pl.pallas_call(
        _rwkv_kernel,
        grid=(bsz, seq // tm),
        in_specs=[tok(2 * WIDTH),
                  pl.BlockSpec((1, nchunk, WIDTH, 2 * CHUNK), lambda b, s: (b, s, 0, 0)),
                  tok(WIDTH), tok(2 * WIDTH),
                  pl.BlockSpec((1, nchunk, 1, WIDTH), lambda b, s: (b, s, 0, 0)),
                  row(WIDTH), row(WIDTH)],
        out_specs=tok(WIDTH),
        out_shape=jax.ShapeDtypeStruct((bsz, seq, WIDTH), jnp.bfloat16),
        scratch_shapes=[pltpu.VMEM((HEADS, HEAD_DIM, HEAD_DIM), jnp.float32),
                        pltpu.VMEM((nb, 2 * CHUNK, HEAD_DIM), jnp.bfloat16),
                        pltpu.VMEM((nb, CHUNK, HEAD_DIM), jnp.bfloat16),
                        pltpu.VMEM((nb, HEAD_DIM, HEAD_DIM), jnp.float32),
                        pltpu.VMEM((nchunk, CHUNK, WIDTH), jnp.float32)],
        compiler_params=pltpu.CompilerParams(
            dimension_semantics=("arbitrary", "arbitrary"), vmem_limit_bytes=VMEM_LIMIT),
        name="rwkv7",
    )(lhs, rhst, vv, gb, wc, ln_w, ln_b)


FF_CHUNK = 256


def _post_kernel(of_ref, or_ref, gate_ref, x_ref, mod_ref, wfo_ref, wro_ref, wo_ref, g2_ref,
                 wup_ref, cw_ref, cb_ref, wdn_ref, fg_ref, o_ref, carry_ref, act_ref):
    s = pl.program_id(1)
    tm = x_ref.shape[1]

    @pl.when(s == 0)
    def _():
        carry_ref[...] = jnp.zeros_like(carry_ref)

    mod = mod_ref[0]
    gate1 = mod[:, 2 * D_MODEL:3 * D_MODEL]
    shift2 = mod[:, 3 * D_MODEL:4 * D_MODEL]
    scale2 = mod[:, 4 * D_MODEL:5 * D_MODEL]
    gate2 = mod[:, 5 * D_MODEL:6 * D_MODEL]

    halves = [slice(i * (tm // 2), (i + 1) * (tm // 2)) for i in range(2)]
    branch = [(jnp.dot(of_ref[0, rs, :], wfo_ref[...], preferred_element_type=jnp.float32),
               jnp.dot(or_ref[0, rs, :], wro_ref[...], preferred_element_type=jnp.float32)) for rs in halves]
    mix = []
    for rs, (m_fox, m_rwkv) in zip(halves, branch):
        gate = _sigmoid(gate_ref[0, rs, :].astype(jnp.float32))
        merged = gate[:, 0:D_MODEL] * m_fox + gate[:, D_MODEL:] * m_rwkv
        mix.append(jnp.dot(merged.astype(jnp.bfloat16), wo_ref[...], preferred_element_type=jnp.float32))
    h1_parts, u2_parts = [], []
    for rs, mx in zip(halves, mix):
        h1_h = x_ref[0, rs, :] + gate1 * mx
        y = h1_h * lax.rsqrt(jnp.mean(h1_h * h1_h, axis=-1, keepdims=True) + RMS_EPS) * g2_ref[...]
        h1_parts.append(h1_h)
        u2_parts.append((y * (1.0 + scale2) + shift2).astype(jnp.bfloat16))
    h1 = jnp.concatenate(h1_parts, axis=0)
    u2 = jnp.concatenate(u2_parts, axis=0)

    row = lax.broadcasted_iota(jnp.int32, (tm, FF_CHUNK), 0)
    def up(c):
        lo = c * FF_CHUNK
        return (jnp.dot(u2, wup_ref[:, lo:lo + FF_CHUNK], preferred_element_type=jnp.float32),
                jnp.dot(u2, wup_ref[:, D_FF + lo:D_FF + lo + FF_CHUNK], preferred_element_type=jnp.float32))

    for c in range(D_FF // FF_CHUNK):
        cs = slice(c * FF_CHUNK, (c + 1) * FF_CHUNK)
        a, b = up(c)
        tail = carry_ref[:, cs]
        p1 = tail[7:8, :]
        p2 = tail[6:7, :]
        a1 = jnp.where(row == 0, p1, pltpu.roll(a, 1, 0))
        a2 = jnp.where(row == 0, p2, jnp.where(row == 1, p1, pltpu.roll(a, 2, 0)))
        carry_ref[:, cs] = a[tm - 8:tm, :]
        cw = cw_ref[:, cs]
        conv = cw[0:1, :] * a2 + cw[1:2, :] * a1 + cw[2:3, :] * a + cb_ref[:, cs]
        act_ref[:, cs] = (conv * _sigmoid(conv) * b).astype(jnp.bfloat16)

    acc = jnp.dot(act_ref[...], wdn_ref[...], preferred_element_type=jnp.float32)
    h2 = h1 + gate2 * acc
    o_ref[0] = h2 * lax.rsqrt(jnp.mean(h2 * h2, axis=-1, keepdims=True) + RMS_EPS) * fg_ref[...]


def _post_call(o_fox, o_rwkv, gates, x, mod3, wfo, wro, wo, ln2_g, wup, conv_w, conv_b, wdn, final_g, tm):
    bsz, seq, _ = x.shape
    tok = lambda n: pl.BlockSpec((1, tm, n), lambda b, s: (b, s, 0))
    const = lambda shape: pl.BlockSpec(shape, lambda b, s: (0,) * len(shape),
                                       pipeline_mode=pl.Buffered(1))
    return pl.pallas_call(
        _post_kernel,
        grid=(bsz, seq // tm),
        in_specs=[tok(WIDTH), tok(WIDTH), tok(2 * D_MODEL), tok(D_MODEL),
                  pl.BlockSpec((1, 1, 6 * D_MODEL), lambda b, s: (b, 0, 0)),
                  const((WIDTH, D_MODEL)), const((WIDTH, D_MODEL)), const((D_MODEL, D_MODEL)),
                  const((1, D_MODEL)), const((D_MODEL, 2 * D_FF)), const((3, D_FF)), const((1, D_FF)),
                  const((D_FF, D_MODEL)), const((1, D_MODEL))],
        out_specs=tok(D_MODEL),
        out_shape=jax.ShapeDtypeStruct((bsz, seq, D_MODEL), jnp.float32),
        scratch_shapes=[pltpu.VMEM((8, D_FF), jnp.float32),
                        pltpu.VMEM((tm, D_FF), jnp.bfloat16)],
        compiler_params=pltpu.CompilerParams(
            dimension_semantics=("arbitrary", "arbitrary"), vmem_limit_bytes=VMEM_LIMIT),
        name="post_ffn",
    )(o_fox, o_rwkv, gates, x, mod3, wfo, wro, wo, ln2_g, wup, conv_w, conv_b, wdn, final_g)


def _pad_cols(w, n):
    return jnp.pad(w, ((0, 0), (0, n - w.shape[1])))


def _layer(h, c, w_mod, b_mod, ln1_g, w_in, fox_b_f, fox_w_out, rwkv_mu, rwkv_w0, rwkv_w2, rwkv_a0,
           rwkv_a2, rwkv_g2, rwkv_k_k, rwkv_k_a, rwkv_r_k, rwkv_ln_w, rwkv_ln_b, rwkv_w_out, w_o,
           ln2_g, w_up, conv_w, conv_b, w_down, out_gain):
    bsz, seq, _ = h.shape
    bf16 = jnp.bfloat16
    lora = (LORA_PAD - 64)

    fox_cols = 3 * WIDTH + HEADS
    o_r = fox_cols
    o_g = o_r + 3 * WIDTH + 64 + 64 + GATE_LORA
    w_in_p = jnp.concatenate([
        w_in[:, 0:3 * WIDTH],
        w_in[:, o_r:o_r + 3 * WIDTH],
        _pad_cols(w_in[:, o_r + 3 * WIDTH:o_r + 3 * WIDTH + 64], LORA_PAD),
        _pad_cols(w_in[:, o_r + 3 * WIDTH + 64:o_r + 3 * WIDTH + 128], LORA_PAD),
        w_in[:, o_r + 3 * WIDTH + 128:o_g],
        _pad_cols(w_in[:, 3 * WIDTH:fox_cols], LANES),
        w_in[:, o_g:]], axis=1).astype(bf16)
    mu = rwkv_mu.reshape(1, -1)
    mu_p = jnp.concatenate([mu[:, 0:3 * WIDTH], _pad_cols(mu[:, 3 * WIDTH:3 * WIDTH + 64], LORA_PAD),
                            _pad_cols(mu[:, 3 * WIDTH + 64:3 * WIDTH + 128], LORA_PAD),
                            mu[:, 3 * WIDTH + 128:], jnp.zeros((1, LANES), mu.dtype)], axis=1)
    w2_p = jnp.pad(rwkv_w2, ((0, lora), (0, 0))).astype(bf16)
    a2_p = jnp.pad(rwkv_a2, ((0, lora), (0, 0))).astype(bf16)
    bf_pad = _pad_cols(fox_b_f.reshape(1, HEADS), LANES)
    rowv = lambda t: t.reshape(1, -1)

    tm_in = min(512, seq)
    tq = min(512, seq)
    tm_rw = min(256, seq)
    tm_post = min(512, seq)

    mod = _mod_call(c, w_mod, b_mod)
    mod3 = mod.reshape(bsz, 1, 6 * D_MODEL)
    rwkv_params = (mu_p, rowv(rwkv_w0), w2_p, rowv(rwkv_a0), a2_p, rwkv_g2.astype(bf16),
                   rowv(rwkv_k_k), rowv(rwkv_k_a), rowv(rwkv_r_k))
    q_aug, k_aug, v_aug, gates, r_lhs, r_rhst, r_v, r_gb, r_wc = _inproj_call(
        h, mod3, rowv(ln1_g), bf_pad, w_in_p, rwkv_params, tm_in)
    o_fox = _fox_call(q_aug, k_aug, v_aug, tq)
    o_rwkv = _rwkv_call(r_lhs, r_rhst, r_v, r_gb, r_wc, rowv(rwkv_ln_w), rowv(rwkv_ln_b), tm_rw)
    return _post_call(o_fox, o_rwkv, gates, h, mod3, fox_w_out.astype(bf16), rwkv_w_out.astype(bf16),
                      w_o.astype(bf16), rowv(ln2_g), w_up.astype(bf16), conv_w, rowv(conv_b),
                      w_down.astype(bf16), rowv(out_gain), tm_post)


def kernel(x, c, w_mod, b_mod, ln1_g, w_in, fox_b_f, fox_w_out, rwkv_mu, rwkv_w0, rwkv_w2, rwkv_a0,
           rwkv_a2, rwkv_g2, rwkv_k_k, rwkv_k_a, rwkv_r_k, rwkv_ln_w, rwkv_ln_b, rwkv_w_out, w_o,
           ln2_g, w_up, conv_w, conv_b, w_down, final_g):
    depth = w_mod.shape[0]
    assert depth == 1, "single trunk layer: the final RMSNorm is fused into the layer's last kernel"
    return _layer(x, c, w_mod[0], b_mod[0], ln1_g[0], w_in[0], fox_b_f[0], fox_w_out[0], rwkv_mu[0],
                  rwkv_w0[0], rwkv_w2[0], rwkv_a0[0], rwkv_a2[0], rwkv_g2[0], rwkv_k_k[0], rwkv_k_a[0],
                  rwkv_r_k[0], rwkv_ln_w[0], rwkv_ln_b[0], rwkv_w_out[0], w_o[0], ln2_g[0], w_up[0],
                  conv_w[0], conv_b[0], w_down[0], final_g)
```

```python
import functools

import jax
import jax.numpy as jnp
from jax import lax
from jax.experimental import pallas as pl
from jax.experimental.pallas import tpu as pltpu

D_MODEL = 1024
HEADS = 8
HEAD_DIM = 64
WIDTH = HEADS * HEAD_DIM
LORA_PAD = 128
GATE_LORA = 128
D_FF = 2816
RMS_EPS = 1e-6
GN_EPS = HEAD_DIM * 1e-5
L2_EPS = 1e-12
LANES = 128
CHUNK = 64

OFF_QKV = 0
OFF_RWKV = 3 * WIDTH
RWKV_P = 3 * WIDTH + 2 * LORA_PAD + GATE_LORA
OFF_F = OFF_RWKV + RWKV_P
OFF_GATE = OFF_F + LANES
IN_P = OFF_GATE + 2 * D_MODEL

VMEM_LIMIT = 56 * 1024 * 1024

HI = lax.Precision.HIGHEST
NT = (((1,), (1,)), ((), ()))


def _bdot(a, b):
    return jnp.dot(a.astype(jnp.bfloat16), b.astype(jnp.bfloat16), preferred_element_type=jnp.float32)


def _sigmoid(x):
    return 1.0 / (1.0 + jnp.exp(-x))


def _softplus(x):
    return jnp.maximum(x, 0.0) + jnp.log(1.0 + jnp.exp(-jnp.abs(x)))


def _tri(n, strict):
    r = lax.broadcasted_iota(jnp.int32, (n, n), 0)
    c = lax.broadcasted_iota(jnp.int32, (n, n), 1)
    return (r > c) if strict else (r >= c)


def _head_sum(x):
    lane = lax.broadcasted_iota(jnp.int32, (x.shape[0], LANES), 1)
    lo = lane < HEAD_DIM
    outs = []
    for g in range(x.shape[1] // LANES):
        xg = x[:, g * LANES:(g + 1) * LANES]
        s0 = jnp.sum(jnp.where(lo, xg, 0.0), axis=-1, keepdims=True)
        s1 = jnp.sum(jnp.where(lo, 0.0, xg), axis=-1, keepdims=True)
        outs.append(jnp.where(lo, s0, s1))
    return jnp.concatenate(outs, axis=-1)


def _mod_kernel(c_ref, w_ref, b_ref, o_ref):
    c = c_ref[...]
    sc = c * _sigmoid(c)
    o_ref[...] = jnp.dot(sc, w_ref[...], precision=HI, preferred_element_type=jnp.float32) + b_ref[...]


def _mod_call(c, w_mod, b_mod):
    bsz = c.shape[0]
    n = w_mod.shape[1]
    tn = 1024
    return pl.pallas_call(
        _mod_kernel,
        grid=(n // tn,),
        in_specs=[pl.BlockSpec((bsz, D_MODEL), lambda j: (0, 0)),
                  pl.BlockSpec((D_MODEL, tn), lambda j: (0, j)),
                  pl.BlockSpec((1, tn), lambda j: (0, j))],
        out_specs=pl.BlockSpec((bsz, tn), lambda j: (0, j)),
        out_shape=jax.ShapeDtypeStruct((bsz, n), jnp.float32),
        name="mod",
    )(c, w_mod, b_mod.reshape(1, n))


def _split3(c):
    f32 = jnp.float32
    hi = c.astype(jnp.bfloat16).astype(f32)
    mid = (c - hi).astype(jnp.bfloat16).astype(f32)
    lo = (c - hi - mid).astype(jnp.bfloat16).astype(f32)
    return hi, mid, lo


def _cumsum_rows(tri, x):
    return sum(jnp.dot(tri, part.astype(jnp.bfloat16), preferred_element_type=jnp.float32)
               for part in _split3(x))


def _inproj_kernel(x_ref, mod_ref, g_ref, bf_ref, w_ref,
                   mu_ref, w0_ref, w2_ref, a0_ref, a2_ref, g2_ref, kk_ref, ka_ref, rk_ref,
                   q_ref, k_ref, v_ref, gate_ref, lhs_ref, rhst_ref, vv_ref, gb_ref, wc_ref,
                   carry_ref, pcarry_ref):
    s = pl.program_id(1)
    tm = x_ref.shape[1]

    @pl.when(s == 0)
    def _():
        carry_ref[...] = jnp.zeros_like(carry_ref)
        pcarry_ref[...] = jnp.zeros_like(pcarry_ref)

    x = x_ref[0]
    mod = mod_ref[0]
    shift = mod[:, 0:D_MODEL]
    scale = mod[:, D_MODEL:2 * D_MODEL]
    y = x * lax.rsqrt(jnp.mean(x * x, axis=-1, keepdims=True) + RMS_EPS) * g_ref[...]
    u = (y * (1.0 + scale) + shift).astype(jnp.bfloat16)

    def proj(off, n):
        return jnp.dot(u, w_ref[:, off:off + n], preferred_element_type=jnp.float32)

    row = lax.broadcasted_iota(jnp.int32, (tm, WIDTH), 0)

    def shifted(off):
        cur = proj(OFF_RWKV + off, WIDTH)
        prev = jnp.where(row == 0, pcarry_ref[:, off:off + WIDTH], pltpu.roll(cur, 1, 0))
        pcarry_ref[:, off:off + WIDTH] = cur[tm - 1:tm, :]
        return cur, cur + (prev - cur) * mu_ref[:, off:off + WIDTH]

    def gate(i):
        gate_ref[0, :, i * 512:(i + 1) * 512] = proj(OFF_GATE + i * 512, 512).astype(jnp.bfloat16)

    _, r = shifted(0)
    _, k_r = shifted(WIDTH)
    _, v_r = shifted(2 * WIDTH)
    tail, tail_l = shifted(3 * WIDTH)
    xw = tail_l[:, 0:LORA_PAD]
    xa = tail_l[:, LORA_PAD:2 * LORA_PAD]
    xg = tail_l[:, 2 * LORA_PAD:2 * LORA_PAD + GATE_LORA]
    gate(0)
    zw = w0_ref[...] + _bdot(jnp.tanh(xw), w2_ref[...])
    lw = -jnp.exp(-_softplus(-zw) - 0.5)
    al = _sigmoid(a0_ref[...] + _bdot(xa, a2_ref[...]))
    gb_ref[0, :, 0:WIDTH] = _bdot(_sigmoid(xg), g2_ref[...])
    gate(1)
    kk = k_r * kk_ref[...]
    kk = kk / jnp.maximum(jnp.sqrt(_head_sum(kk * kk)), L2_EPS)
    gate(2)
    k_mod = k_r * (1.0 + (al - 1.0) * ka_ref[...])
    b_vec = kk * al
    gb_ref[0, :, WIDTH:] = _head_sum(r * k_mod * rk_ref[...]) * v_r
    vv_ref[0] = v_r.astype(jnp.bfloat16)

    z = tail[:, RWKV_P - 3 * WIDTH:] + bf_ref[...]
    logf = jnp.minimum(z, 0.0) - jnp.log(1.0 + jnp.exp(-jnp.abs(z)))
    f_hi, f_mid, f_lo = _split3(logf)
    lane_f = lax.broadcasted_iota(jnp.int32, (tm, LANES), 1)
    packed = jnp.where(lane_f < HEADS, f_hi,
                       jnp.where(lane_f < 2 * HEADS, pltpu.roll(f_mid, HEADS, 1),
                                 jnp.where(lane_f < 3 * HEADS, pltpu.roll(f_lo, 2 * HEADS, 1), 0.0)))
    part = jnp.dot(_tri(tm, strict=False).astype(jnp.bfloat16), packed.astype(jnp.bfloat16),
                   preferred_element_type=jnp.float32)
    cum = (part + pltpu.roll(part, LANES - HEADS, 1)) + pltpu.roll(part, LANES - 2 * HEADS, 1) + carry_ref[...]
    carry_ref[...] = cum[tm - 1:tm, :]

    tri = _tri(CHUNK, strict=False).astype(jnp.bfloat16)

    def chunk(c):
        sl = slice(c * CHUNK, (c + 1) * CHUNK)
        lwc = lw[sl]
        cl = _cumsum_rows(tri, lwc)
        el = jnp.exp(cl)
        inv = jnp.exp(-cl)
        lhs_ref[0, sl, 0:WIDTH] = (-kk[sl] * jnp.exp(cl - lwc)).astype(jnp.bfloat16)
        lhs_ref[0, sl, WIDTH:] = (r[sl] * el).astype(jnp.bfloat16)
        rhs = jnp.concatenate([b_vec[sl] * inv, k_mod[sl] * inv], axis=0)
        rhst_ref[0, c] = rhs.T.astype(jnp.bfloat16)
        wc_ref[0, c] = el[CHUNK - 1:CHUNK, :]

    nck = tm // CHUNK
    cuts = [0, (nck + 2) // 3, (2 * nck + 2) // 3, nck]
    q = proj(OFF_QKV, WIDTH) * (HEAD_DIM ** -0.5)
    for c in range(cuts[0], cuts[1]):
        chunk(c)
    k = proj(OFF_QKV + WIDTH, WIDTH)
    for c in range(cuts[1], cuts[2]):
        chunk(c)
    v = proj(OFF_QKV + 2 * WIDTH, WIDTH)
    for c in range(cuts[2], cuts[3]):
        chunk(c)
    gate(3)
    lane = lax.broadcasted_iota(jnp.int32, (tm, LANES), 1)
    lo = lane < HEAD_DIM
    for h in range(HEADS):
        g = slice((h // 2) * LANES, (h // 2 + 1) * LANES)
        take = (lambda t: t[:, g]) if h % 2 == 0 else (lambda t: pltpu.roll(t[:, g], HEAD_DIM, 1))
        c_hi, c_mid, c_lo = _split3(cum[:, h:h + 1])
        parts = jnp.where(lane == HEAD_DIM, c_hi, jnp.where(lane == HEAD_DIM + 1, c_mid, c_lo))
        q_bias = jnp.where(lane < HEAD_DIM + 3, parts, jnp.where(lane < HEAD_DIM + 6, -1.0, 0.0))
        parts = jnp.where(lane == HEAD_DIM + 3, c_hi, jnp.where(lane == HEAD_DIM + 4, c_mid, c_lo))
        k_bias = jnp.where(lane < HEAD_DIM + 3, 1.0, jnp.where(lane < HEAD_DIM + 6, parts, 0.0))
        v_one = jnp.where(lane == HEAD_DIM, 1.0, 0.0)
        q_ref[0, h] = jnp.where(lo, take(q), q_bias).astype(jnp.bfloat16)
        k_ref[0, h] = jnp.where(lo, take(k), k_bias).astype(jnp.bfloat16)
        v_ref[0, h] = jnp.where(lo, take(v), v_one).astype(jnp.bfloat16)


def _inproj_call(x, mod3, ln1_g, bf_pad, w_in_p, rwkv_params, tm):
    bsz, seq, _ = x.shape
    grid = (bsz, seq // tm)
    nck = tm // CHUNK
    tok = lambda n: pl.BlockSpec((1, tm, n), lambda b, s: (b, s, 0))
    const = lambda shape: pl.BlockSpec(shape, lambda b, s: (0,) * len(shape))
    head = pl.BlockSpec((1, HEADS, tm, LANES), lambda b, s: (b, 0, s, 0))
    out_shapes = (
        jax.ShapeDtypeStruct((bsz, HEADS, seq, LANES), jnp.bfloat16),
        jax.ShapeDtypeStruct((bsz, HEADS, seq, LANES), jnp.bfloat16),
        jax.ShapeDtypeStruct((bsz, HEADS, seq, LANES), jnp.bfloat16),
        jax.ShapeDtypeStruct((bsz, seq, 2 * D_MODEL), jnp.bfloat16),
        jax.ShapeDtypeStruct((bsz, seq, 2 * WIDTH), jnp.bfloat16),
        jax.ShapeDtypeStruct((bsz, seq // CHUNK, WIDTH, 2 * CHUNK), jnp.bfloat16),
        jax.ShapeDtypeStruct((bsz, seq, WIDTH), jnp.bfloat16),
        jax.ShapeDtypeStruct((bsz, seq, 2 * WIDTH), jnp.float32),
        jax.ShapeDtypeStruct((bsz, seq // CHUNK, 1, WIDTH), jnp.float32),
    )
    return pl.pallas_call(
        _inproj_kernel,
        grid=grid,
        in_specs=[tok(D_MODEL),
                  pl.BlockSpec((1, 1, 6 * D_MODEL), lambda b, s: (b, 0, 0)),
                  const((1, D_MODEL)), const((1, LANES)), const((D_MODEL, IN_P))]
                 + [const(t.shape) for t in rwkv_params],
        out_specs=(head, head, head, tok(2 * D_MODEL), tok(2 * WIDTH),
                   pl.BlockSpec((1, nck, WIDTH, 2 * CHUNK), lambda b, s: (b, s, 0, 0)),
                   tok(WIDTH), tok(2 * WIDTH),
                   pl.BlockSpec((1, nck, 1, WIDTH), lambda b, s: (b, s, 0, 0))),
        out_shape=out_shapes,
        scratch_shapes=[pltpu.VMEM((1, LANES), jnp.float32),
                        pltpu.VMEM((1, 4 * WIDTH), jnp.float32)],
        compiler_params=pltpu.CompilerParams(
            dimension_semantics=("arbitrary", "arbitrary"), vmem_limit_bytes=VMEM_LIMIT),
        name="in_proj",
    )(x, mod3, ln1_g, bf_pad, w_in_p, *rwkv_params)


NEG = -1e30


def _fox_kernel(q_ref, k_ref, v_ref, o_ref, sa_ref, sb_ref, acc_ref, *, tq):
    nt = q_ref.shape[2] // tq
    n_items = nt * (nt + 1) // 2
    causal = _tri(tq, strict=False)

    def logits(i, j, s_ref):
        i0 = pl.multiple_of(i * tq, tq)
        j0 = pl.multiple_of(j * tq, tq)
        for hh in range(2):
            s_ref[hh] = lax.dot_general(q_ref[0, hh, pl.ds(i0, tq), :], k_ref[0, hh, pl.ds(j0, tq), :], NT,
                                        preferred_element_type=jnp.float32)

    def advance(i, j):
        inner = j < i
        return jnp.minimum(jnp.where(inner, i, i + 1), nt - 1), jnp.where(inner, j + 1, 0)

    def item(i, j, s_ref, carry):
        j0 = pl.multiple_of(j * tq, tq)
        first = j == 0
        keep = jnp.logical_or(causal, j < i)
        new = []
        for hh in range(2):
            m, acc = carry[hh]
            m = jnp.where(first, NEG, m)
            acc = jnp.where(first, 0.0, acc)
            sc = jnp.where(keep, s_ref[hh], NEG)
            m_new = jnp.maximum(m, jnp.max(sc, axis=-1, keepdims=True))
            p = jnp.exp(sc - m_new).astype(jnp.bfloat16)
            acc = jnp.exp(m - m_new) * acc + jnp.dot(p, v_ref[0, hh, pl.ds(j0, tq), :],
                                                     preferred_element_type=jnp.float32)
            acc_ref[i, hh] = acc
            new.append((m_new, acc))
        return tuple(new)

    per_trip = next(u for u in (12, 6, 4, 2) if n_items % u == 0)
    bufs = (sa_ref, sb_ref)

    def trip(t, state):
        i, j, carry = state
        for u in range(per_trip):
            i_next, j_next = advance(i, j)
            logits(i_next, j_next, bufs[(u + 1) % 2])
            carry = item(i, j, bufs[u % 2], carry)
            i, j = i_next, j_next
        return i, j, carry

    init = tuple((jnp.full((tq, 1), NEG, jnp.float32), jnp.zeros((tq, LANES), jnp.float32))
                 for _ in range(2))
    zero = jnp.zeros((), jnp.int32)
    logits(zero, zero, sa_ref)
    lax.fori_loop(0, n_items // per_trip, trip, (zero, zero, init))

    lane = lax.broadcasted_iota(jnp.int32, (tq, LANES), 1)
    for i in range(nt):
        acc0 = acc_ref[i, 0]
        acc1 = acc_ref[i, 1]
        o0 = acc0 / acc0[:, HEAD_DIM:HEAD_DIM + 1]
        o1 = pltpu.roll(acc1 / acc1[:, HEAD_DIM:HEAD_DIM + 1], HEAD_DIM, 1)
        o_ref[0, i * tq:(i + 1) * tq, :] = jnp.where(lane < HEAD_DIM, o0, o1).astype(o_ref.dtype)


def _fox_call(q_aug, k_aug, v_aug, tq):
    bsz, _, seq, _ = q_aug.shape
    npair = HEADS // 2
    nt = seq // tq
    assert (nt * (nt + 1) // 2) % 2 == 0, "the flat (tile, block) loop runs two items per trip"
    pair_spec = pl.BlockSpec((1, 2, seq, LANES), lambda b, p: (b, p, 0, 0))
    return pl.pallas_call(
        functools.partial(_fox_kernel, tq=tq),
        grid=(bsz, npair),
        in_specs=[pair_spec, pair_spec, pair_spec],
        out_specs=pl.BlockSpec((1, seq, LANES), lambda b, p: (b, 0, p)),
        out_shape=jax.ShapeDtypeStruct((bsz, seq, WIDTH), jnp.bfloat16),
        scratch_shapes=[pltpu.VMEM((2, tq, tq), jnp.float32), pltpu.VMEM((2, tq, tq), jnp.float32),
                        pltpu.VMEM((nt, 2, tq, LANES), jnp.float32)],
        compiler_params=pltpu.CompilerParams(
            dimension_semantics=("arbitrary", "arbitrary"), vmem_limit_bytes=VMEM_LIMIT),
        name="fox_attn",
    )(q_aug, k_aug, v_aug)


BMM = (((2,), (1,)), ((0,), (0,)))


def _bmm(a, b):
    return lax.dot_general(a.astype(jnp.bfloat16), b.astype(jnp.bfloat16), BMM,
                           preferred_element_type=jnp.float32)


def _rwkv_kernel(lhs_in, rhst_in, vv_in, gb_in, wc_in, lnw_ref, lnb_ref, o_ref,
                 st_ref, lhs_ref, v_ref, wc_ref, y_ref):
    s = pl.program_id(1)
    tm = lhs_in.shape[1]
    nchunk = tm // CHUNK
    nb = nchunk * HEADS

    @pl.when(s == 0)
    def _():
        st_ref[...] = jnp.zeros_like(st_ref)

    for c in range(nchunk):
        sl = slice(c * CHUNK, (c + 1) * CHUNK)
        lhs_c = lhs_in[0, sl, :]
        vc = vv_in[0, sl, :]
        wct = jnp.broadcast_to(wc_in[0, c], (8, WIDTH)).T[:, 0:1]
        for h in range(HEADS):
            i = c * HEADS + h
            hs = slice(h * HEAD_DIM, (h + 1) * HEAD_DIM)
            lhs_ref[i, 0:CHUNK, :] = lhs_c[:, hs]
            lhs_ref[i, CHUNK:, :] = lhs_c[:, WIDTH + h * HEAD_DIM:WIDTH + (h + 1) * HEAD_DIM]
            v_ref[i] = vc[:, hs]
            wc_ref[i] = jnp.broadcast_to(wct[hs, :], (HEAD_DIM, HEAD_DIM))
    rhst = rhst_in[0].reshape(nb, HEAD_DIM, 2 * CHUNK)
    g = gb_in[0, :, 0:WIDTH]
    bonus = gb_in[0, :, WIDTH:]

    r2 = lax.broadcasted_iota(jnp.int32, (2 * CHUNK, 2 * CHUNK), 0)
    c2 = lax.broadcasted_iota(jnp.int32, (2 * CHUNK, 2 * CHUNK), 1)
    tt = r2 & (CHUNK - 1)
    jj = c2 & (CHUNK - 1)
    aa_mask = (tt > jj) | ((r2 >= CHUNK) & (tt == jj))
    eye = (_tri(CHUNK, False) & ~_tri(CHUNK, True)).astype(jnp.float32)
    ri = lax.broadcasted_iota(jnp.int32, (CHUNK, CHUNK), 0)
    ci = lax.broadcasted_iota(jnp.int32, (CHUNK, CHUNK), 1)
    lvl_masks = [((ri >> (l + 1)) == (ci >> (l + 1))) & ((ri >> l) != (ci >> l))
                 for l in range(CHUNK.bit_length() - 1)]

    lhs = lhs_ref[...]
    aa = jnp.where(aa_mask, _bmm(lhs, rhst), 0.0)
    a_ab = aa[:, 0:CHUNK, 0:CHUNK]
    a_ak = aa[:, 0:CHUNK, CHUNK:]
    a_r = aa[:, CHUNK:, :]
    t = eye + jnp.where(lvl_masks[0], a_ab, 0.0)
    for lm in lvl_masks[1:]:
        t = t + _bmm(t, _bmm(jnp.where(lm, a_ab, 0.0), t))
    vb = v_ref[...]
    x0 = _bmm(a_ak, vb)
    tt = _bmm(t, jnp.concatenate([lhs[:, 0:CHUNK, :], x0.astype(jnp.bfloat16)], axis=-1))
    ta = tt[:, :, 0:HEAD_DIM]
    tx = tt[:, :, HEAD_DIM:]
    yl = jnp.concatenate([lhs[:, CHUNK:, :], a_r.astype(jnp.bfloat16)], axis=-1)

    st = st_ref[...]
    for c in range(nchunk):
        blk = slice(c * HEADS, (c + 1) * HEADS)
        sl = slice(c * CHUNK, (c + 1) * CHUNK)
        stb = st.astype(jnp.bfloat16)
        u = (_bmm(ta[blk], stb) + tx[blk]).astype(jnp.bfloat16)
        uv = jnp.concatenate([u, vb[blk]], axis=1)
        st = wc_ref[blk] * (st + _bmm(rhst[blk], uv))
        y = _bmm(yl[blk], jnp.concatenate([stb, uv], axis=1))
        for h in range(HEADS):
            y_ref[c, :, h * HEAD_DIM:(h + 1) * HEAD_DIM] = y[h]
        yv = y_ref[c]
        mean = _head_sum(yv) * (1.0 / HEAD_DIM)
        d = yv - mean
        var = _head_sum(d * d) * (1.0 / HEAD_DIM)
        yn = d * lax.rsqrt(var + GN_EPS) * lnw_ref[...] + lnb_ref[...]
        o_ref[0, sl, :] = ((yn + bonus[sl]) * g[sl]).astype(o_ref.dtype)
    st_ref[...] = st


def _rwkv_call(lhs, rhst, vv, gb, wc, ln_w, ln_b, tm):
    bsz, seq, _ = lhs.shape
    nchunk = tm // CHUNK
    nb = nchunk * HEADS
    tok = lambda n: pl.BlockSpec((1, tm, n), lambda b, s: (b, s, 0))
    row = lambda n: pl.BlockSpec((1, n), lambda b, s: (0, 0))
    return pl.pallas_call(
        _rwkv_kernel,
        grid=(bsz, seq // tm),
        in_specs=[tok(2 * WIDTH),
                  pl.BlockSpec((1, nchunk, WIDTH, 2 * CHUNK), lambda b, s: (b, s, 0, 0)),
                  tok(WIDTH), tok(2 * WIDTH),
                  pl.BlockSpec((1, nchunk, 1, WIDTH), lambda b, s: (b, s, 0, 0)),
                  row(WIDTH), row(WIDTH)],
        out_specs=tok(WIDTH),
        out_shape=jax.ShapeDtypeStruct((bsz, seq, WIDTH), jnp.bfloat16),
        scratch_shapes=[pltpu.VMEM((HEADS, HEAD_DIM, HEAD_DIM), jnp.float32),
                        pltpu.VMEM((nb, 2 * CHUNK, HEAD_DIM), jnp.bfloat16),
                        pltpu.VMEM((nb, CHUNK, HEAD_DIM), jnp.bfloat16),
                        pltpu.VMEM((nb, HEAD_DIM, HEAD_DIM), jnp.float32),
                        pltpu.VMEM((nchunk, CHUNK, WIDTH), jnp.float32)],
        compiler_params=pltpu.CompilerParams(
            dimension_semantics=("arbitrary", "arbitrary"), vmem_limit_bytes=VMEM_LIMIT),
        name="rwkv7",
    )(lhs, rhst, vv, gb, wc, ln_w, ln_b)


FF_CHUNK = 256


def _post_kernel(of_ref, or_ref, gate_ref, x_ref, mod_ref, wfo_ref, wro_ref, wo_ref, g2_ref,
                 wup_ref, cw_ref, cb_ref, wdn_ref, fg_ref, o_ref, carry_ref, act_ref):
    s = pl.program_id(1)
    tm = x_ref.shape[1]

    @pl.when(s == 0)
    def _():
        carry_ref[...] = jnp.zeros_like(carry_ref)

    mod = mod_ref[0]
    gate1 = mod[:, 2 * D_MODEL:3 * D_MODEL]
    shift2 = mod[:, 3 * D_MODEL:4 * D_MODEL]
    scale2 = mod[:, 4 * D_MODEL:5 * D_MODEL]
    gate2 = mod[:, 5 * D_MODEL:6 * D_MODEL]

    halves = [slice(i * (tm // 2), (i + 1) * (tm // 2)) for i in range(2)]
    branch = [(jnp.dot(of_ref[0, rs, :], wfo_ref[...], preferred_element_type=jnp.float32),
               jnp.dot(or_ref[0, rs, :], wro_ref[...], preferred_element_type=jnp.float32)) for rs in halves]
    mix = []
    for rs, (m_fox, m_rwkv) in zip(halves, branch):
        gate = _sigmoid(gate_ref[0, rs, :].astype(jnp.float32))
        merged = gate[:, 0:D_MODEL] * m_fox + gate[:, D_MODEL:] * m_rwkv
        mix.append(jnp.dot(merged.astype(jnp.bfloat16), wo_ref[...], preferred_element_type=jnp.float32))
    h1_parts, u2_parts = [], []
    for rs, mx in zip(halves, mix):
        h1_h = x_ref[0, rs, :] + gate1 * mx
        y = h1_h * lax.rsqrt(jnp.mean(h1_h * h1_h, axis=-1, keepdims=True) + RMS_EPS) * g2_ref[...]
        h1_parts.append(h1_h)
        u2_parts.append((y * (1.0 + scale2) + shift2).astype(jnp.bfloat16))
    h1 = jnp.concatenate(h1_parts, axis=0)
    u2 = jnp.concatenate(u2_parts, axis=0)

    row = lax.broadcasted_iota(jnp.int32, (tm, FF_CHUNK), 0)
    def up(c):
        lo = c * FF_CHUNK
        return (jnp.dot(u2, wup_ref[:, lo:lo + FF_CHUNK], preferred_element_type=jnp.float32),
                jnp.dot(u2, wup_ref[:, D_FF + lo:D_FF + lo + FF_CHUNK], preferred_element_type=jnp.float32))

    for c in range(D_FF // FF_CHUNK):
        cs = slice(c * FF_CHUNK, (c + 1) * FF_CHUNK)
        a, b = up(c)
        tail = carry_ref[:, cs]
        p1 = tail[7:8, :]
        p2 = tail[6:7, :]
        a1 = jnp.where(row == 0, p1, pltpu.roll(a, 1, 0))
        a2 = jnp.where(row == 0, p2, jnp.where(row == 1, p1, pltpu.roll(a, 2, 0)))
        carry_ref[:, cs] = a[tm - 8:tm, :]
        cw = cw_ref[:, cs]
        conv = cw[0:1, :] * a2 + cw[1:2, :] * a1 + cw[2:3, :] * a + cb_ref[:, cs]
        act_ref[:, cs] = (conv * _sigmoid(conv) * b).astype(jnp.bfloat16)

    acc = jnp.dot(act_ref[...], wdn_ref[...], preferred_element_type=jnp.float32)
    h2 = h1 + gate2 * acc
    o_ref[0] = h2 * lax.rsqrt(jnp.mean(h2 * h2, axis=-1, keepdims=True) + RMS_EPS) * fg_ref[...]


def _post_call(o_fox, o_rwkv, gates, x, mod3, wfo, wro, wo, ln2_g, wup, conv_w, conv_b, wdn, final_g, tm):
    bsz, seq, _ = x.shape
    tok = lambda n: pl.BlockSpec((1, tm, n), lambda b, s: (b, s, 0))
    const = lambda shape: pl.BlockSpec(shape, lambda b, s: (0,) * len(shape),
                                       pipeline_mode=pl.Buffered(1))
    return pl.pallas_call(
        _post_kernel,
        grid=(bsz, seq // tm),
        in_specs=[tok(WIDTH), tok(WIDTH), tok(2 * D_MODEL), tok(D_MODEL),
                  pl.BlockSpec((1, 1, 6 * D_MODEL), lambda b, s: (b, 0, 0)),
                  const((WIDTH, D_MODEL)), const((WIDTH, D_MODEL)), const((D_MODEL, D_MODEL)),
                  const((1, D_MODEL)), const((D_MODEL, 2 * D_FF)), const((3, D_FF)), const((1, D_FF)),
                  const((D_FF, D_MODEL)), const((1, D_MODEL))],
        out_specs=tok(D_MODEL),
        out_shape=jax.ShapeDtypeStruct((bsz, seq, D_MODEL), jnp.float32),
        scratch_shapes=[pltpu.VMEM((8, D_FF), jnp.float32),
                        pltpu.VMEM((tm, D_FF), jnp.bfloat16)],
        compiler_params=pltpu.CompilerParams(
            dimension_semantics=("arbitrary", "arbitrary"), vmem_limit_bytes=VMEM_LIMIT),
        name="post_ffn",
    )(o_fox, o_rwkv, gates, x, mod3, wfo, wro, wo, ln2_g, wup, conv_w, conv_b, wdn, final_g)


def _pad_cols(w, n):
    return jnp.pad(w, ((0, 0), (0, n - w.shape[1])))


def _layer(h, c, w_mod, b_mod, ln1_g, w_in, fox_b_f, fox_w_out, rwkv_mu, rwkv_w0, rwkv_w2, rwkv_a0,
           rwkv_a2, rwkv_g2, rwkv_k_k, rwkv_k_a, rwkv_r_k, rwkv_ln_w, rwkv_ln_b, rwkv_w_out, w_o,
           ln2_g, w_up, conv_w, conv_b, w_down, out_gain):
    bsz, seq, _ = h.shape
    bf16 = jnp.bfloat16
    lora = (LORA_PAD - 64)

    fox_cols = 3 * WIDTH + HEADS
    o_r = fox_cols
    o_g = o_r + 3 * WIDTH + 64 + 64 + GATE_LORA
    w_in_p = jnp.concatenate([
        w_in[:, 0:3 * WIDTH],
        w_in[:, o_r:o_r + 3 * WIDTH],
        _pad_cols(w_in[:, o_r + 3 * WIDTH:o_r + 3 * WIDTH + 64], LORA_PAD),
        _pad_cols(w_in[:, o_r + 3 * WIDTH + 64:o_r + 3 * WIDTH + 128], LORA_PAD),
        w_in[:, o_r + 3 * WIDTH + 128:o_g],
        _pad_cols(w_in[:, 3 * WIDTH:fox_cols], LANES),
        w_in[:, o_g:]], axis=1).astype(bf16)
    mu = rwkv_mu.reshape(1, -1)
    mu_p = jnp.concatenate([mu[:, 0:3 * WIDTH], _pad_cols(mu[:, 3 * WIDTH:3 * WIDTH + 64], LORA_PAD),
                            _pad_cols(mu[:, 3 * WIDTH + 64:3 * WIDTH + 128], LORA_PAD),
                            mu[:, 3 * WIDTH + 128:], jnp.zeros((1, LANES), mu.dtype)], axis=1)
    w2_p = jnp.pad(rwkv_w2, ((0, lora), (0, 0))).astype(bf16)
    a2_p = jnp.pad(rwkv_a2, ((0, lora), (0, 0))).astype(bf16)
    bf_pad = _pad_cols(fox_b_f.reshape(1, HEADS), LANES)
    rowv = lambda t: t.reshape(1, -1)

    tm_in = min(512, seq)
    tq = min(512, seq)
    tm_rw = min(512, seq)
    tm_post = min(512, seq)

    mod = _mod_call(c, w_mod, b_mod)
    mod3 = mod.reshape(bsz, 1, 6 * D_MODEL)
    rwkv_params = (mu_p, rowv(rwkv_w0), w2_p, rowv(rwkv_a0), a2_p, rwkv_g2.astype(bf16),
                   rowv(rwkv_k_k), rowv(rwkv_k_a), rowv(rwkv_r_k))
    q_aug, k_aug, v_aug, gates, r_lhs, r_rhst, r_v, r_gb, r_wc = _inproj_call(
        h, mod3, rowv(ln1_g), bf_pad, w_in_p, rwkv_params, tm_in)
    o_fox = _fox_call(q_aug, k_aug, v_aug, tq)
    o_rwkv = _rwkv_call(r_lhs, r_rhst, r_v, r_gb, r_wc, rowv(rwkv_ln_w), rowv(rwkv_ln_b), tm_rw)
    return _post_call(o_fox, o_rwkv, gates, h, mod3, fox_w_out.astype(bf16), rwkv_w_out.astype(bf16),
                      w_o.astype(bf16), rowv(ln2_g), w_up.astype(bf16), conv_w, rowv(conv_b),
                      w_down.astype(bf16), rowv(out_gain), tm_post)


def kernel(x, c, w_mod, b_mod, ln1_g, w_in, fox_b_f, fox_w_out, rwkv_mu, rwkv_w0, rwkv_w2, rwkv_a0,
           rwkv_a2, rwkv_g2, rwkv_k_k, rwkv_k_a, rwkv_r_k, rwkv_ln_w, rwkv_ln_b, rwkv_w_out, w_o,
           ln2_g, w_up, conv_w, conv_b, w_down, final_g):
    depth = w_mod.shape[0]
    assert depth == 1, "single trunk layer: the final RMSNorm is fused into the layer's last kernel"
    return _layer(x, c, w_mod[0], b_mod[0], ln1_g[0], w_in[0], fox_b_f[0], fox_w_out[0], rwkv_mu[0],
                  rwkv_w0[0], rwkv_w2[0], rwkv_a0[0], rwkv_a2[0], rwkv_g2[0], rwkv_k_k[0], rwkv_k_a[0],
                  rwkv_r_k[0], rwkv_ln_w[0], rwkv_ln_b[0], rwkv_w_out[0], w_o[0], ln2_g[0], w_up[0],
                  conv_w[0], conv_b[0], w_down[0], final_g)
```

```python
import functools

import jax
import jax.numpy as jnp
from jax import lax
from jax.experimental import pallas as pl
from jax.experimental.pallas import tpu as pltpu

D_MODEL = 1024
HEADS = 8
HEAD_DIM = 64
WIDTH = HEADS * HEAD_DIM
LORA_PAD = 128
GATE_LORA = 128
D_FF = 2816
RMS_EPS = 1e-6
GN_EPS = HEAD_DIM * 1e-5
L2_EPS = 1e-12
LANES = 128
CHUNK = 64

OFF_QKV = 0
OFF_RWKV = 3 * WIDTH
RWKV_P = 3 * WIDTH + 2 * LORA_PAD + GATE_LORA
OFF_F = OFF_RWKV + RWKV_P
OFF_GATE = OFF_F + LANES
IN_P = OFF_GATE + 2 * D_MODEL

VMEM_LIMIT = 56 * 1024 * 1024

HI = lax.Precision.HIGHEST
NT = (((1,), (1,)), ((), ()))


def _bdot(a, b):
    return jnp.dot(a.astype(jnp.bfloat16), b.astype(jnp.bfloat16), preferred_element_type=jnp.float32)


def _sigmoid(x):
    return 1.0 / (1.0 + jnp.exp(-x))


def _softplus(x):
    return jnp.maximum(x, 0.0) + jnp.log(1.0 + jnp.exp(-jnp.abs(x)))


def _tri(n, strict):
    r = lax.broadcasted_iota(jnp.int32, (n, n), 0)
    c = lax.broadcasted_iota(jnp.int32, (n, n), 1)
    return (r > c) if strict else (r >= c)


def _head_sum(x):
    lane = lax.broadcasted_iota(jnp.int32, (x.shape[0], LANES), 1)
    lo = lane < HEAD_DIM
    outs = []
    for g in range(x.shape[1] // LANES):
        xg = x[:, g * LANES:(g + 1) * LANES]
        s0 = jnp.sum(jnp.where(lo, xg, 0.0), axis=-1, keepdims=True)
        s1 = jnp.sum(jnp.where(lo, 0.0, xg), axis=-1, keepdims=True)
        outs.append(jnp.where(lo, s0, s1))
    return jnp.concatenate(outs, axis=-1)


def _mod_kernel(c_ref, w_ref, b_ref, o_ref):
    c = c_ref[...]
    sc = c * _sigmoid(c)
    o_ref[...] = jnp.dot(sc, w_ref[...], precision=HI, preferred_element_type=jnp.float32) + b_ref[...]


def _mod_call(c, w_mod, b_mod):
    bsz = c.shape[0]
    n = w_mod.shape[1]
    tn = 1024
    return pl.pallas_call(
        _mod_kernel,
        grid=(n // tn,),
        in_specs=[pl.BlockSpec((bsz, D_MODEL), lambda j: (0, 0)),
                  pl.BlockSpec((D_MODEL, tn), lambda j: (0, j)),
                  pl.BlockSpec((1, tn), lambda j: (0, j))],
        out_specs=pl.BlockSpec((bsz, tn), lambda j: (0, j)),
        out_shape=jax.ShapeDtypeStruct((bsz, n), jnp.float32),
        name="mod",
    )(c, w_mod, b_mod.reshape(1, n))


def _split3(c):
    f32 = jnp.float32
    hi = c.astype(jnp.bfloat16).astype(f32)
    mid = (c - hi).astype(jnp.bfloat16).astype(f32)
    lo = (c - hi - mid).astype(jnp.bfloat16).astype(f32)
    return hi, mid, lo


def _cumsum_rows(tri, x):
    return sum(jnp.dot(tri, part.astype(jnp.bfloat16), preferred_element_type=jnp.float32)
               for part in _split3(x))


def _inproj_kernel(x_ref, mod_ref, g_ref, bf_ref, w_ref,
                   mu_ref, w0_ref, w2_ref, a0_ref, a2_ref, g2_ref, kk_ref, ka_ref, rk_ref,
                   q_ref, k_ref, v_ref, gate_ref, lhs_ref, rhst_ref, vv_ref, gb_ref, wc_ref,
                   carry_ref, pcarry_ref):
    s = pl.program_id(1)
    tm = x_ref.shape[1]

    @pl.when(s == 0)
    def _():
        carry_ref[...] = jnp.zeros_like(carry_ref)
        pcarry_ref[...] = jnp.zeros_like(pcarry_ref)

    x = x_ref[0]
    mod = mod_ref[0]
    shift = mod[:, 0:D_MODEL]
    scale = mod[:, D_MODEL:2 * D_MODEL]
    y = x * lax.rsqrt(jnp.mean(x * x, axis=-1, keepdims=True) + RMS_EPS) * g_ref[...]
    u = (y * (1.0 + scale) + shift).astype(jnp.bfloat16)

    def proj(off, n):
        return jnp.dot(u, w_ref[:, off:off + n], preferred_element_type=jnp.float32)

    row = lax.broadcasted_iota(jnp.int32, (tm, WIDTH), 0)

    def shifted(off):
        cur = proj(OFF_RWKV + off, WIDTH)
        prev = jnp.where(row == 0, pcarry_ref[:, off:off + WIDTH], pltpu.roll(cur, 1, 0))
        pcarry_ref[:, off:off + WIDTH] = cur[tm - 1:tm, :]
        return cur, cur + (prev - cur) * mu_ref[:, off:off + WIDTH]

    def gate(i):
        gate_ref[0, :, i * 512:(i + 1) * 512] = proj(OFF_GATE + i * 512, 512).astype(jnp.bfloat16)

    _, r = shifted(0)
    _, k_r = shifted(WIDTH)
    _, v_r = shifted(2 * WIDTH)
    tail, tail_l = shifted(3 * WIDTH)
    xw = tail_l[:, 0:LORA_PAD]
    xa = tail_l[:, LORA_PAD:2 * LORA_PAD]
    xg = tail_l[:, 2 * LORA_PAD:2 * LORA_PAD + GATE_LORA]
    gate(0)
    zw = w0_ref[...] + _bdot(jnp.tanh(xw), w2_ref[...])
    lw = -jnp.exp(-_softplus(-zw) - 0.5)
    al = _sigmoid(a0_ref[...] + _bdot(xa, a2_ref[...]))
    gb_ref[0, :, 0:WIDTH] = _bdot(_sigmoid(xg), g2_ref[...])
    gate(1)
    kk = k_r * kk_ref[...]
    kk = kk / jnp.maximum(jnp.sqrt(_head_sum(kk * kk)), L2_EPS)
    gate(2)
    k_mod = k_r * (1.0 + (al - 1.0) * ka_ref[...])
    b_vec = kk * al
    gb_ref[0, :, WIDTH:] = _head_sum(r * k_mod * rk_ref[...]) * v_r
    vv_ref[0] = v_r.astype(jnp.bfloat16)

    z = tail[:, RWKV_P - 3 * WIDTH:] + bf_ref[...]
    logf = jnp.minimum(z, 0.0) - jnp.log(1.0 + jnp.exp(-jnp.abs(z)))
    f_hi, f_mid, f_lo = _split3(logf)
    lane_f = lax.broadcasted_iota(jnp.int32, (tm, LANES), 1)
    packed = jnp.where(lane_f < HEADS, f_hi,
                       jnp.where(lane_f < 2 * HEADS, pltpu.roll(f_mid, HEADS, 1),
                                 jnp.where(lane_f < 3 * HEADS, pltpu.roll(f_lo, 2 * HEADS, 1), 0.0)))
    part = jnp.dot(_tri(tm, strict=False).astype(jnp.bfloat16), packed.astype(jnp.bfloat16),
                   preferred_element_type=jnp.float32)
    cum = (part + pltpu.roll(part, LANES - HEADS, 1)) + pltpu.roll(part, LANES - 2 * HEADS, 1) + carry_ref[...]
    carry_ref[...] = cum[tm - 1:tm, :]

    tri = _tri(CHUNK, strict=False).astype(jnp.bfloat16)

    def chunk(c):
        sl = slice(c * CHUNK, (c + 1) * CHUNK)
        lwc = lw[sl]
        cl = _cumsum_rows(tri, lwc)
        el = jnp.exp(cl)
        inv = jnp.exp(-cl)
        lhs_ref[0, sl, 0:WIDTH] = (-kk[sl] * jnp.exp(cl - lwc)).astype(jnp.bfloat16)
        lhs_ref[0, sl, WIDTH:] = (r[sl] * el).astype(jnp.bfloat16)
        rhs = jnp.concatenate([b_vec[sl] * inv, k_mod[sl] * inv], axis=0)
        rhst_ref[0, c] = rhs.T.astype(jnp.bfloat16)
        wc_ref[0, c] = el[CHUNK - 1:CHUNK, :]

    nck = tm // CHUNK
    cuts = [0, (nck + 2) // 3, (2 * nck + 2) // 3, nck]
    q = proj(OFF_QKV, WIDTH) * (HEAD_DIM ** -0.5)
    for c in range(cuts[0], cuts[1]):
        chunk(c)
    k = proj(OFF_QKV + WIDTH, WIDTH)
    for c in range(cuts[1], cuts[2]):
        chunk(c)
    v = proj(OFF_QKV + 2 * WIDTH, WIDTH)
    for c in range(cuts[2], cuts[3]):
        chunk(c)
    gate(3)
    lane = lax.broadcasted_iota(jnp.int32, (tm, LANES), 1)
    lo = lane < HEAD_DIM
    c_hi, c_mid, c_lo = _split3(cum)
    packed_c = jnp.where(lane < HEADS, c_hi,
                         jnp.where(lane < 2 * HEADS, pltpu.roll(c_mid, HEADS, 1),
                                   jnp.where(lane < 3 * HEADS, pltpu.roll(c_lo, 2 * HEADS, 1), 0.0)))
    q_at = HEAD_DIM
    k_at = HEAD_DIM + 3 * HEADS
    sel_q = (lane == q_at) | (lane == q_at + HEADS) | (lane == q_at + 2 * HEADS)
    sel_k = (lane == k_at) | (lane == k_at + HEADS) | (lane == k_at + 2 * HEADS)
    q_const = jnp.where(sel_k, -1.0, 0.0)
    k_const = jnp.where(sel_q, 1.0, 0.0)
    v_one = jnp.where(lane == HEAD_DIM, 1.0, 0.0)
    for h in range(HEADS):
        g = slice((h // 2) * LANES, (h // 2 + 1) * LANES)
        take = (lambda t: t[:, g]) if h % 2 == 0 else (lambda t: pltpu.roll(t[:, g], HEAD_DIM, 1))
        q_bias = jnp.where(sel_q, pltpu.roll(packed_c, q_at - h, 1), q_const)
        k_bias = jnp.where(sel_k, pltpu.roll(packed_c, k_at - h, 1), k_const)
        q_ref[0, h] = jnp.where(lo, take(q), q_bias).astype(jnp.bfloat16)
        k_ref[0, h] = jnp.where(lo, take(k), k_bias).astype(jnp.bfloat16)
        v_ref[0, h] = jnp.where(lo, take(v), v_one).astype(jnp.bfloat16)


def _inproj_call(x, mod3, ln1_g, bf_pad, w_in_p, rwkv_params, tm):
    bsz, seq, _ = x.shape
    grid = (bsz, seq // tm)
    nck = tm // CHUNK
    tok = lambda n: pl.BlockSpec((1, tm, n), lambda b, s: (b, s, 0))
    const = lambda shape: pl.BlockSpec(shape, lambda b, s: (0,) * len(shape))
    head = pl.BlockSpec((1, HEADS, tm, LANES), lambda b, s: (b, 0, s, 0))
    out_shapes = (
        jax.ShapeDtypeStruct((bsz, HEADS, seq, LANES), jnp.bfloat16),
        jax.ShapeDtypeStruct((bsz, HEADS, seq, LANES), jnp.bfloat16),
        jax.ShapeDtypeStruct((bsz, HEADS, seq, LANES), jnp.bfloat16),
        jax.ShapeDtypeStruct((bsz, seq, 2 * D_MODEL), jnp.bfloat16),
        jax.ShapeDtypeStruct((bsz, seq, 2 * WIDTH), jnp.bfloat16),
        jax.ShapeDtypeStruct((bsz, seq // CHUNK, WIDTH, 2 * CHUNK), jnp.bfloat16),
        jax.ShapeDtypeStruct((bsz, seq, WIDTH), jnp.bfloat16),
        jax.ShapeDtypeStruct((bsz, seq, 2 * WIDTH), jnp.float32),
        jax.ShapeDtypeStruct((bsz, seq // CHUNK, 1, WIDTH), jnp.float32),
    )
    return pl.pallas_call(
        _inproj_kernel,
        grid=grid,
        in_specs=[tok(D_MODEL),
                  pl.BlockSpec((1, 1, 6 * D_MODEL), lambda b, s: (b, 0, 0)),
                  const((1, D_MODEL)), const((1, LANES)), const((D_MODEL, IN_P))]
                 + [const(t.shape) for t in rwkv_params],
        out_specs=(head, head, head, tok(2 * D_MODEL), tok(2 * WIDTH),
                   pl.BlockSpec((1, nck, WIDTH, 2 * CHUNK), lambda b, s: (b, s, 0, 0)),
                   tok(WIDTH), tok(2 * WIDTH),
                   pl.BlockSpec((1, nck, 1, WIDTH), lambda b, s: (b, s, 0, 0))),
        out_shape=out_shapes,
        scratch_shapes=[pltpu.VMEM((1, LANES), jnp.float32),
                        pltpu.VMEM((1, 4 * WIDTH), jnp.float32)],
        compiler_params=pltpu.CompilerParams(
            dimension_semantics=("arbitrary", "arbitrary"), vmem_limit_bytes=VMEM_LIMIT),
        name="in_proj",
    )(x, mod3, ln1_g, bf_pad, w_in_p, *rwkv_params)


NEG = -1e30


def _fox_kernel(q_ref, k_ref, v_ref, o_ref, sa_ref, sb_ref, acc_ref, *, tq):
    nt = q_ref.shape[2] // tq
    n_items = nt * (nt + 1) // 2
    causal = _tri(tq, strict=False)

    def logits(i, j, s_ref):
        i0 = pl.multiple_of(i * tq, tq)
        j0 = pl.multiple_of(j * tq, tq)
        for hh in range(2):
            s_ref[hh] = lax.dot_general(q_ref[0, hh, pl.ds(i0, tq), :], k_ref[0, hh, pl.ds(j0, tq), :], NT,
                                        preferred_element_type=jnp.float32)

    def advance(i, j):
        inner = j < i
        return jnp.minimum(jnp.where(inner, i, i + 1), nt - 1), jnp.where(inner, j + 1, 0)

    def item(i, j, s_ref, carry):
        j0 = pl.multiple_of(j * tq, tq)
        first = j == 0
        keep = jnp.logical_or(causal, j < i)
        new = []
        for hh in range(2):
            m, acc = carry[hh]
            m = jnp.where(first, NEG, m)
            acc = jnp.where(first, 0.0, acc)
            sc = jnp.where(keep, s_ref[hh], NEG)
            m_new = jnp.maximum(m, jnp.max(sc, axis=-1, keepdims=True))
            p = jnp.exp(sc - m_new).astype(jnp.bfloat16)
            acc = jnp.exp(m - m_new) * acc + jnp.dot(p, v_ref[0, hh, pl.ds(j0, tq), :],
                                                     preferred_element_type=jnp.float32)
            acc_ref[i, hh] = acc
            new.append((m_new, acc))
        return tuple(new)

    per_trip = next(u for u in (12, 6, 4, 2) if n_items % u == 0)
    bufs = (sa_ref, sb_ref)

    def trip(t, state):
        i, j, carry = state
        for u in range(per_trip):
            i_next, j_next = advance(i, j)
            logits(i_next, j_next, bufs[(u + 1) % 2])
            carry = item(i, j, bufs[u % 2], carry)
            i, j = i_next, j_next
        return i, j, carry

    init = tuple((jnp.full((tq, 1), NEG, jnp.float32), jnp.zeros((tq, LANES), jnp.float32))
                 for _ in range(2))
    zero = jnp.zeros((), jnp.int32)
    logits(zero, zero, sa_ref)
    lax.fori_loop(0, n_items // per_trip, trip, (zero, zero, init))

    lane = lax.broadcasted_iota(jnp.int32, (tq, LANES), 1)
    for i in range(nt):
        acc0 = acc_ref[i, 0]
        acc1 = acc_ref[i, 1]
        o0 = acc0 / acc0[:, HEAD_DIM:HEAD_DIM + 1]
        o1 = pltpu.roll(acc1 / acc1[:, HEAD_DIM:HEAD_DIM + 1], HEAD_DIM, 1)
        o_ref[0, i * tq:(i + 1) * tq, :] = jnp.where(lane < HEAD_DIM, o0, o1).astype(o_ref.dtype)


def _fox_call(q_aug, k_aug, v_aug, tq):
    bsz, _, seq, _ = q_aug.shape
    npair = HEADS // 2
    nt = seq // tq
    assert (nt * (nt + 1) // 2) % 2 == 0, "the flat (tile, block) loop runs two items per trip"
    pair_spec = pl.BlockSpec((1, 2, seq, LANES), lambda b, p: (b, p, 0, 0))
    return pl.pallas_call(
        functools.partial(_fox_kernel, tq=tq),
        grid=(bsz, npair),
        in_specs=[pair_spec, pair_spec, pair_spec],
        out_specs=pl.BlockSpec((1, seq, LANES), lambda b, p: (b, 0, p)),
        out_shape=jax.ShapeDtypeStruct((bsz, seq, WIDTH), jnp.bfloat16),
        scratch_shapes=[pltpu.VMEM((2, tq, tq), jnp.float32), pltpu.VMEM((2, tq, tq), jnp.float32),
                        pltpu.VMEM((nt, 2, tq, LANES), jnp.float32)],
        compiler_params=pltpu.CompilerParams(
            dimension_semantics=("arbitrary", "arbitrary"), vmem_limit_bytes=VMEM_LIMIT),
        name="fox_attn",
    )(q_aug, k_aug, v_aug)


BMM = (((2,), (1,)), ((0,), (0,)))


def _bmm(a, b):
    return lax.dot_general(a.astype(jnp.bfloat16), b.astype(jnp.bfloat16), BMM,
                           preferred_element_type=jnp.float32)


def _rwkv_kernel(lhs_in, rhst_in, vv_in, gb_in, wc_in, lnw_ref, lnb_ref, o_ref,
                 st_ref, lhs_ref, v_ref, wc_ref, y_ref):
    s = pl.program_id(1)
    tm = lhs_in.shape[1]
    nchunk = tm // CHUNK
    nb = nchunk * HEADS

    @pl.when(s == 0)
    def _():
        st_ref[...] = jnp.zeros_like(st_ref)

    for c in range(nchunk):
        sl = slice(c * CHUNK, (c + 1) * CHUNK)
        lhs_c = lhs_in[0, sl, :]
        vc = vv_in[0, sl, :]
        wct = jnp.broadcast_to(wc_in[0, c], (8, WIDTH)).T[:, 0:1]
        for h in range(HEADS):
            i = c * HEADS + h
            hs = slice(h * HEAD_DIM, (h + 1) * HEAD_DIM)
            lhs_ref[i, 0:CHUNK, :] = lhs_c[:, hs]
            lhs_ref[i, CHUNK:, :] = lhs_c[:, WIDTH + h * HEAD_DIM:WIDTH + (h + 1) * HEAD_DIM]
            v_ref[i] = vc[:, hs]
            wc_ref[i] = jnp.broadcast_to(wct[hs, :], (HEAD_DIM, HEAD_DIM))
    rhst = rhst_in[0].reshape(nb, HEAD_DIM, 2 * CHUNK)
    g = gb_in[0, :, 0:WIDTH]
    bonus = gb_in[0, :, WIDTH:]

    r2 = lax.broadcasted_iota(jnp.int32, (2 * CHUNK, 2 * CHUNK), 0)
    c2 = lax.broadcasted_iota(jnp.int32, (2 * CHUNK, 2 * CHUNK), 1)
    tt = r2 & (CHUNK - 1)
    jj = c2 & (CHUNK - 1)
    aa_mask = (tt > jj) | ((r2 >= CHUNK) & (tt == jj))
    eye = (_tri(CHUNK, False) & ~_tri(CHUNK, True)).astype(jnp.float32)
    ri = lax.broadcasted_iota(jnp.int32, (CHUNK, CHUNK), 0)
    ci = lax.broadcasted_iota(jnp.int32, (CHUNK, CHUNK), 1)
    lvl_masks = [((ri >> (l + 1)) == (ci >> (l + 1))) & ((ri >> l) != (ci >> l))
                 for l in range(CHUNK.bit_length() - 1)]

    lhs = lhs_ref[...]
    aa = jnp.where(aa_mask, _bmm(lhs, rhst), 0.0)
    a_ab = aa[:, 0:CHUNK, 0:CHUNK]
    a_ak = aa[:, 0:CHUNK, CHUNK:]
    a_r = aa[:, CHUNK:, :]
    t = eye + jnp.where(lvl_masks[0], a_ab, 0.0)
    for lm in lvl_masks[1:]:
        t = t + _bmm(t, _bmm(jnp.where(lm, a_ab, 0.0), t))
    vb = v_ref[...]
    x0 = _bmm(a_ak, vb)
    tt = _bmm(t, jnp.concatenate([lhs[:, 0:CHUNK, :], x0.astype(jnp.bfloat16)], axis=-1))
    ta = tt[:, :, 0:HEAD_DIM]
    tx = tt[:, :, HEAD_DIM:]
    yl = jnp.concatenate([lhs[:, CHUNK:, :], a_r.astype(jnp.bfloat16)], axis=-1)

    st = st_ref[...]
    for c in range(nchunk):
        blk = slice(c * HEADS, (c + 1) * HEADS)
        sl = slice(c * CHUNK, (c + 1) * CHUNK)
        stb = st.astype(jnp.bfloat16)
        u = (_bmm(ta[blk], stb) + tx[blk]).astype(jnp.bfloat16)
        uv = jnp.concatenate([u, vb[blk]], axis=1)
        st = wc_ref[blk] * (st + _bmm(rhst[blk], uv))
        y = _bmm(yl[blk], jnp.concatenate([stb, uv], axis=1))
        for h in range(HEADS):
            y_ref[c, :, h * HEAD_DIM:(h + 1) * HEAD_DIM] = y[h]
        yv = y_ref[c]
        mean = _head_sum(yv) * (1.0 / HEAD_DIM)
        d = yv - mean
        var = _head_sum(d * d) * (1.0 / HEAD_DIM)
        yn = d * lax.rsqrt(var + GN_EPS) * lnw_ref[...] + lnb_ref[...]
        o_ref[0, sl, :] = ((yn + bonus[sl]) * g[sl]).astype(o_ref.dtype)
    st_ref[...] = st


def _rwkv_call(lhs, rhst, vv, gb, wc, ln_w, ln_b, tm):
    bsz, seq, _ = lhs.shape
    nchunk = tm // CHUNK
    nb = nchunk * HEADS
    tok = lambda n: pl.BlockSpec((1, tm, n), lambda b, s: (b, s, 0))
    row = lambda n: pl.BlockSpec((1, n), lambda b, s: (0, 0))
    return pl.pallas_call(
        _rwkv_kernel,
        grid=(bsz, seq // tm),
        in_specs=[tok(2 * WIDTH),
                  pl.BlockSpec((1, nchunk, WIDTH, 2 * CHUNK), lambda b, s: (b, s, 0, 0)),
                  tok(WIDTH), tok(2 * WIDTH),
                  pl.BlockSpec((1, nchunk, 1, WIDTH), lambda b, s: (b, s, 0, 0)),
                  row(WIDTH), row(WIDTH)],
        out_specs=tok(WIDTH),
        out_shape=jax.ShapeDtypeStruct((bsz, seq, WIDTH), jnp.bfloat16),
        scratch_shapes=[pltpu.VMEM((HEADS, HEAD_DIM, HEAD_DIM), jnp.float32),
                        pltpu.VMEM((nb, 2 * CHUNK, HEAD_DIM), jnp.bfloat16),
                        pltpu.VMEM((nb, CHUNK, HEAD_DIM), jnp.bfloat16),
                        pltpu.VMEM((nb, HEAD_DIM, HEAD_DIM), jnp.float32),
                        pltpu.VMEM((nchunk, CHUNK, WIDTH), jnp.float32)],
        compiler_params=pltpu.CompilerParams(
            dimension_semantics=("arbitrary", "arbitrary"), vmem_limit_bytes=VMEM_LIMIT),
        name="rwkv7",
    )(lhs, rhst, vv, gb, wc, ln_w, ln_b)


FF_CHUNK = 256


def _post_kernel(of_ref, or_ref, gate_ref, x_ref, mod_ref, wfo_ref, wro_ref, wo_ref, g2_ref,
                 wup_ref, cw_ref, cb_ref, wdn_ref, fg_ref, o_ref, carry_ref, act_ref):
    s = pl.program_id(1)
    tm = x_ref.shape[1]

    @pl.when(s == 0)
    def _():
        carry_ref[...] = jnp.zeros_like(carry_ref)

    mod = mod_ref[0]
    gate1 = mod[:, 2 * D_MODEL:3 * D_MODEL]
    shift2 = mod[:, 3 * D_MODEL:4 * D_MODEL]
    scale2 = mod[:, 4 * D_MODEL:5 * D_MODEL]
    gate2 = mod[:, 5 * D_MODEL:6 * D_MODEL]

    halves = [slice(i * (tm // 2), (i + 1) * (tm // 2)) for i in range(2)]
    branch = [(jnp.dot(of_ref[0, rs, :], wfo_ref[...], preferred_element_type=jnp.float32),
               jnp.dot(or_ref[0, rs, :], wro_ref[...], preferred_element_type=jnp.float32)) for rs in halves]
    mix = []
    for rs, (m_fox, m_rwkv) in zip(halves, branch):
        gate = _sigmoid(gate_ref[0, rs, :].astype(jnp.float32))
        merged = gate[:, 0:D_MODEL] * m_fox + gate[:, D_MODEL:] * m_rwkv
        mix.append(jnp.dot(merged.astype(jnp.bfloat16), wo_ref[...], preferred_element_type=jnp.float32))
    h1_parts, u2_parts = [], []
    for rs, mx in zip(halves, mix):
        h1_h = x_ref[0, rs, :] + gate1 * mx
        y = h1_h * lax.rsqrt(jnp.mean(h1_h * h1_h, axis=-1, keepdims=True) + RMS_EPS) * g2_ref[...]
        h1_parts.append(h1_h)
        u2_parts.append((y * (1.0 + scale2) + shift2).astype(jnp.bfloat16))
    h1 = jnp.concatenate(h1_parts, axis=0)
    u2 = jnp.concatenate(u2_parts, axis=0)

    row = lax.broadcasted_iota(jnp.int32, (tm, FF_CHUNK), 0)
    def up(c):
        lo = c * FF_CHUNK
        return (jnp.dot(u2, wup_ref[:, lo:lo + FF_CHUNK], preferred_element_type=jnp.float32),
                jnp.dot(u2, wup_ref[:, D_FF + lo:D_FF + lo + FF_CHUNK], preferred_element_type=jnp.float32))

    for c in range(D_FF // FF_CHUNK):
        cs = slice(c * FF_CHUNK, (c + 1) * FF_CHUNK)
        a, b = up(c)
        tail = carry_ref[:, cs]
        p1 = tail[7:8, :]
        p2 = tail[6:7, :]
        a1 = jnp.where(row == 0, p1, pltpu.roll(a, 1, 0))
        a2 = jnp.where(row == 0, p2, jnp.where(row == 1, p1, pltpu.roll(a, 2, 0)))
        carry_ref[:, cs] = a[tm - 8:tm, :]
        cw = cw_ref[:, cs]
        conv = cw[0:1, :] * a2 + cw[1:2, :] * a1 + cw[2:3, :] * a + cb_ref[:, cs]
        act_ref[:, cs] = (conv * _sigmoid(conv) * b).astype(jnp.bfloat16)

    acc = jnp.dot(act_ref[...], wdn_ref[...], preferred_element_type=jnp.float32)
    h2 = h1 + gate2 * acc
    o_ref[0] = h2 * lax.rsqrt(jnp.mean(h2 * h2, axis=-1, keepdims=True) + RMS_EPS) * fg_ref[...]


def _post_call(o_fox, o_rwkv, gates, x, mod3, wfo, wro, wo, ln2_g, wup, conv_w, conv_b, wdn, final_g, tm):
    bsz, seq, _ = x.shape
    tok = lambda n: pl.BlockSpec((1, tm, n), lambda b, s: (b, s, 0))
    const = lambda shape: pl.BlockSpec(shape, lambda b, s: (0,) * len(shape),
                                       pipeline_mode=pl.Buffered(1))
    return pl.pallas_call(
        _post_kernel,
        grid=(bsz, seq // tm),
        in_specs=[tok(WIDTH), tok(WIDTH), tok(2 * D_MODEL), tok(D_MODEL),
                  pl.BlockSpec((1, 1, 6 * D_MODEL), lambda b, s: (b, 0, 0)),
                  const((WIDTH, D_MODEL)), const((WIDTH, D_MODEL)), const((D_MODEL, D_MODEL)),
                  const((1, D_MODEL)), const((D_MODEL, 2 * D_FF)), const((3, D_FF)), const((1, D_FF)),
                  const((D_FF, D_MODEL)), const((1, D_MODEL))],
        out_specs=tok(D_MODEL),
        out_shape=jax.ShapeDtypeStruct((bsz, seq, D_MODEL), jnp.float32),
        scratch_shapes=[pltpu.VMEM((8, D_FF), jnp.float32),
                        pltpu.VMEM((tm, D_FF), jnp.bfloat16)],
        compiler_params=pltpu.CompilerParams(
            dimension_semantics=("arbitrary", "arbitrary"), vmem_limit_bytes=VMEM_LIMIT),
        name="post_ffn",
    )(o_fox, o_rwkv, gates, x, mod3, wfo, wro, wo, ln2_g, wup, conv_w, conv_b, wdn, final_g)


def _pad_cols(w, n):
    return jnp.pad(w, ((0, 0), (0, n - w.shape[1])))


def _layer(h, c, w_mod, b_mod, ln1_g, w_in, fox_b_f, fox_w_out, rwkv_mu, rwkv_w0, rwkv_w2, rwkv_a0,
           rwkv_a2, rwkv_g2, rwkv_k_k, rwkv_k_a, rwkv_r_k, rwkv_ln_w, rwkv_ln_b, rwkv_w_out, w_o,
           ln2_g, w_up, conv_w, conv_b, w_down, out_gain):
    bsz, seq, _ = h.shape
    bf16 = jnp.bfloat16
    lora = (LORA_PAD - 64)

    fox_cols = 3 * WIDTH + HEADS
    o_r = fox_cols
    o_g = o_r + 3 * WIDTH + 64 + 64 + GATE_LORA
    w_in_p = jnp.concatenate([
        w_in[:, 0:3 * WIDTH],
        w_in[:, o_r:o_r + 3 * WIDTH],
        _pad_cols(w_in[:, o_r + 3 * WIDTH:o_r + 3 * WIDTH + 64], LORA_PAD),
        _pad_cols(w_in[:, o_r + 3 * WIDTH + 64:o_r + 3 * WIDTH + 128], LORA_PAD),
        w_in[:, o_r + 3 * WIDTH + 128:o_g],
        _pad_cols(w_in[:, 3 * WIDTH:fox_cols], LANES),
        w_in[:, o_g:]], axis=1).astype(bf16)
    mu = rwkv_mu.reshape(1, -1)
    mu_p = jnp.concatenate([mu[:, 0:3 * WIDTH], _pad_cols(mu[:, 3 * WIDTH:3 * WIDTH + 64], LORA_PAD),
                            _pad_cols(mu[:, 3 * WIDTH + 64:3 * WIDTH + 128], LORA_PAD),
                            mu[:, 3 * WIDTH + 128:], jnp.zeros((1, LANES), mu.dtype)], axis=1)
    w2_p = jnp.pad(rwkv_w2, ((0, lora), (0, 0))).astype(bf16)
    a2_p = jnp.pad(rwkv_a2, ((0, lora), (0, 0))).astype(bf16)
    bf_pad = _pad_cols(fox_b_f.reshape(1, HEADS), LANES)
    rowv = lambda t: t.reshape(1, -1)

    tm_in = min(512, seq)
    tq = min(512, seq)
    tm_rw = min(512, seq)
    tm_post = min(512, seq)

    mod = _mod_call(c, w_mod, b_mod)
    mod3 = mod.reshape(bsz, 1, 6 * D_MODEL)
    rwkv_params = (mu_p, rowv(rwkv_w0), w2_p, rowv(rwkv_a0), a2_p, rwkv_g2.astype(bf16),
                   rowv(rwkv_k_k), rowv(rwkv_k_a), rowv(rwkv_r_k))
    q_aug, k_aug, v_aug, gates, r_lhs, r_rhst, r_v, r_gb, r_wc = _inproj_call(
        h, mod3, rowv(ln1_g), bf_pad, w_in_p, rwkv_params, tm_in)
    o_fox = _fox_call(q_aug, k_aug, v_aug, tq)
    o_rwkv = _rwkv_call(r_lhs, r_rhst, r_v, r_gb, r_wc, rowv(rwkv_ln_w), rowv(rwkv_ln_b), tm_rw)
    return _post_call(o_fox, o_rwkv, gates, h, mod3, fox_w_out.astype(bf16), rwkv_w_out.astype(bf16),
                      w_o.astype(bf16), rowv(ln2_g), w_up.astype(bf16), conv_w, rowv(conv_b),
                      w_down.astype(bf16), rowv(out_gain), tm_post)


def kernel(x, c, w_mod, b_mod, ln1_g, w_in, fox_b_f, fox_w_out, rwkv_mu, rwkv_w0, rwkv_w2, rwkv_a0,
           rwkv_a2, rwkv_g2, rwkv_k_k, rwkv_k_a, rwkv_r_k, rwkv_ln_w, rwkv_ln_b, rwkv_w_out, w_o,
           ln2_g, w_up, conv_w, conv_b, w_down, final_g):
    depth = w_mod.shape[0]
    assert depth == 1, "single trunk layer: the final RMSNorm is fused into the layer's last kernel"
    return _layer(x, c, w_mod[0], b_mod[0], ln1_g[0], w_in[0], fox_b_f[0], fox_w_out[0], rwkv_mu[0],
                  rwkv_w0[0], rwkv_w2[0], rwkv_a0[0], rwkv_a2[0], rwkv_g2[0], rwkv_k_k[0], rwkv_k_a[0],
                  rwkv_r_k[0], rwkv_ln_w[0], rwkv_ln_b[0], rwkv_w_out[0], w_o[0], ln2_g[0], w_up[0],
                  conv_w[0], conv_b[0], w_down[0], final_g)
```
